```python
import math
import jax, jax.numpy as jnp
from jax import lax
import numpy as np

D_MODEL = 2048
BATCH = 1
SEQ = 8192
DEPTH = 2
DEC_BATCH = 32
DEC_SEQ = 8
PAST_LEN = 8192
PAGE_SIZE = 128

N_AB_LAYERS = (DEPTH + 1) // 2
N_ATT_LAYERS = DEPTH // 2

D_INNER_A = D_MODEL
HEADDIM_A = 64
N_HEADS_A = D_INNER_A // HEADDIM_A
N_GROUPS_A = 4
D_STATE = 128
CONV_W = 4
CONV_DIM = D_INNER_A + 2 * N_GROUPS_A * D_STATE
CHUNK_A = 128

W_B = D_MODEL
CHUNK_B = 128
N_GROUPS_B = 8

IN_AB = D_INNER_A + CONV_DIM + N_HEADS_A + 2 * W_B
MIX_AB = D_INNER_A + W_B

N_HEADS_C = 8
HEAD_DIM_C = 128
ATT_W = N_HEADS_C * 2 * HEAD_DIM_C
Q_BLOCK = 128

NUM_BUCKETS = 32
MAX_DISTANCE = 128

D_FF = ((8 * D_MODEL // 3 + 255) // 256) * 256

EPS = 1e-6

kernel_name = 'hybrid_ssd_gmlp_diffattn_decode_step'


def rmsnorm(x, g, eps=EPS):
    xf = x.astype(jnp.float32)
    y = xf * lax.rsqrt(jnp.mean(xf * xf, axis=-1, keepdims=True) + eps)
    return (y * g.astype(jnp.float32)).astype(x.dtype)


def layernorm(x, g, b, eps=1e-5):
    xf = x.astype(jnp.float32)
    mu = jnp.mean(xf, axis=-1, keepdims=True)
    xc = xf - mu
    var = jnp.mean(xc * xc, axis=-1, keepdims=True)
    return (xc * lax.rsqrt(var + eps) * g.astype(jnp.float32) + b.astype(jnp.float32)).astype(x.dtype)


def gated_group_rmsnorm(y, z, w, groups, eps=1e-5):
    g = (y * jax.nn.silu(z)).astype(jnp.float32)
    gs = g.reshape(g.shape[:-1] + (groups, g.shape[-1] // groups))
    gs = gs * lax.rsqrt(jnp.mean(gs * gs, axis=-1, keepdims=True) + eps)
    return (gs.reshape(g.shape) * w.astype(jnp.float32)).astype(y.dtype)


def causal_depthwise_conv(xbc, buf, w, b):
    T = xbc.shape[1]
    xp = jnp.concatenate([buf.astype(xbc.dtype), xbc], axis=1)
    out = b
    for tap in range(CONV_W):
        out = out + w[tap] * xp[:, tap:tap + T]
    return out, xp[:, T:]


def ssd_scan(x, dt, A, Bm, Cm, h0):
    bsz, T, H, P = x.shape
    G, N = Bm.shape[2], Bm.shape[3]
    E = H // G
    L = min(CHUNK_A, T)
    n_chunks = -(-T // L)
    pad = n_chunks * L - T
    f32 = jnp.float32

    def padt(a):
        return jnp.pad(a, [(0, 0), (0, pad)] + [(0, 0)] * (a.ndim - 2))

    x = padt(x.astype(f32)).reshape(bsz, n_chunks, L, G, E, P)
    dt = padt(dt.astype(f32)).reshape(bsz, n_chunks, L, G, E)
    Bm = padt(Bm.astype(f32)).reshape(bsz, n_chunks, L, G, N)
    Cm = padt(Cm.astype(f32)).reshape(bsz, n_chunks, L, G, N)
    a_cum = jnp.cumsum(dt * A.astype(f32).reshape(G, E), axis=2)
    xdt = x * dt[..., None]
    seg = a_cum[:, :, :, None] - a_cum[:, :, None, :]
    causal = jnp.tril(jnp.ones((L, L), dtype=bool))[:, :, None, None]
    decay = jnp.exp(jnp.where(causal, seg, -jnp.inf))
    cb = jnp.einsum('bctgn,bcsgn->bctsg', Cm, Bm)
    y_diag = jnp.einsum('bctsg,bctsge,bcsgep->bctgep', cb, decay, xdt)
    decay_end = jnp.exp(a_cum[:, :, -1:] - a_cum)
    chunk_states = jnp.einsum('bcsgn,bcsge,bcsgep->bcgepn', Bm, decay_end, xdt)
    chunk_decay = jnp.exp(a_cum[:, :, -1])

    def step(h, inp):
        st, dec = inp
        return dec[..., None, None] * h + st, h

    h_last, h_prev = lax.scan(step, h0.astype(f32).reshape(bsz, G, E, P, N),
                              (jnp.moveaxis(chunk_states, 1, 0), jnp.moveaxis(chunk_decay, 1, 0)))
    h_prev = jnp.moveaxis(h_prev, 0, 1)
    y_off = jnp.einsum('bctgn,bcgepn,bctge->bctgep', Cm, h_prev, jnp.exp(a_cum))
    y = (y_diag + y_off).reshape(bsz, n_chunks * L, H, P)[:, :T]
    return y, h_last.reshape(bsz, H, P, N)


def chunk_spatial_mix(vn, w_s, b_s):
    bsz, T, W = vn.shape
    n_chunks = -(-T // CHUNK_B)
    vp = jnp.pad(vn, [(0, 0), (0, n_chunks * CHUNK_B - T), (0, 0)])
    vp = vp.reshape(bsz, n_chunks, CHUNK_B, N_GROUPS_B, W // N_GROUPS_B)
    w_causal = w_s * jnp.tril(jnp.ones((CHUNK_B, CHUNK_B), w_s.dtype))
    mix = jnp.einsum('gts,bcsgd->bctgd', w_causal, vp) + b_s.T[:, :, None]
    return mix.reshape(bsz, n_chunks * CHUNK_B, W)[:, :T]


def ab_mixer(hn, conv_buf, ssm_h0, w_in, conv_w, conv_b, dt_bias, a_log, d_skip, gnorm_w,
             ln_v_g, ln_v_b, w_s, b_s, w_out):
    bsz, T, _ = hn.shape
    proj = hn @ w_in
    splits = np.cumsum([D_INNER_A, CONV_DIM, N_HEADS_A, W_B]).tolist()
    z, xbc, dt_raw, u, v = jnp.split(proj, splits, axis=-1)
    xbc, new_buf = causal_depthwise_conv(xbc, conv_buf, conv_w, conv_b)
    xbc = jax.nn.silu(xbc)
    xs, Bm, Cm = jnp.split(xbc, [D_INNER_A, D_INNER_A + N_GROUPS_A * D_STATE], axis=-1)
    xs = xs.reshape(bsz, T, N_HEADS_A, HEADDIM_A)
    Bm = Bm.reshape(bsz, T, N_GROUPS_A, D_STATE)
    Cm = Cm.reshape(bsz, T, N_GROUPS_A, D_STATE)
    dt = jax.nn.softplus((dt_raw + dt_bias).astype(jnp.float32))
    A = -jnp.exp(a_log.astype(jnp.float32))
    y, h_last = ssd_scan(xs, dt, A, Bm, Cm, ssm_h0)
    y = (y + d_skip.astype(jnp.float32)[:, None] * xs.astype(jnp.float32)).astype(hn.dtype)
    ya = gated_group_rmsnorm(y.reshape(bsz, T, D_INNER_A), z, gnorm_w, N_GROUPS_A)
    u = jax.nn.gelu(u, approximate=False)
    v = jax.nn.gelu(v, approximate=False)
    vn = layernorm(v, ln_v_g, ln_v_b)
    yb = u * chunk_spatial_mix(vn, w_s, b_s)
    out = jnp.concatenate([ya, yb], axis=-1) @ w_out
    return out, new_buf, h_last.astype(ssm_h0.dtype), vn


def t5_bucket(rel):
    n = jnp.maximum(rel, 0)
    max_exact = NUM_BUCKETS // 2
    nf = jnp.maximum(n, 1).astype(jnp.float32)
    large = max_exact + (jnp.log(nf / max_exact) / math.log(MAX_DISTANCE / max_exact)
                         * (NUM_BUCKETS - max_exact)).astype(jnp.int32)
    large = jnp.minimum(large, NUM_BUCKETS - 1)
    return jnp.where(n < max_exact, n, large)


def diff_attn_core(q, k, v, q_pos, k_pos, rel_bias, lam):
    s = jnp.einsum('bqhcd,bkhcd->bhcqk', q, k, preferred_element_type=jnp.float32) * HEAD_DIM_C ** -0.5
    rel = q_pos[:, None] - k_pos[None, :]
    bias = jnp.moveaxis(rel_bias[t5_bucket(rel)].astype(jnp.float32), -1, 0)
    s = jnp.where(rel >= 0, s + bias[None, :, None], -jnp.inf)
    p = jax.nn.softmax(s, axis=-1)
    w = p[:, :, 0] - lam * p[:, :, 1]
    return jnp.einsum('bhqk,bkhe->bqhe', w.astype(v.dtype), v)


def lambda_init(layer_idx):
    return 0.8 - 0.6 * math.exp(-0.3 * layer_idx)


def diff_attention(hn, w_qkv, lq1, lk1, lq2, lk2, subln_w, w_o, rel_bias, lam_init, past):
    bsz, T, _ = hn.shape
    q, k, v = jnp.split(hn @ w_qkv, 3, axis=-1)
    q = q.reshape(bsz, T, N_HEADS_C, 2, HEAD_DIM_C)
    k = k.reshape(bsz, T, N_HEADS_C, 2, HEAD_DIM_C)
    v = v.reshape(bsz, T, N_HEADS_C, 2 * HEAD_DIM_C)
    f32 = jnp.float32
    lam = (jnp.exp(jnp.sum(lq1.astype(f32) * lk1.astype(f32)))
           - jnp.exp(jnp.sum(lq2.astype(f32) * lk2.astype(f32))) + lam_init)
    if past is None:
        k_pos = jnp.arange(T)

        def q_block(i):
            qb = lax.dynamic_slice_in_dim(q, i * Q_BLOCK, Q_BLOCK, axis=1)
            return diff_attn_core(qb, k, v, i * Q_BLOCK + jnp.arange(Q_BLOCK), k_pos, rel_bias, lam)

        o = lax.map(q_block, jnp.arange(T // Q_BLOCK))
        o = jnp.moveaxis(o, 0, 1).reshape(bsz, T, N_HEADS_C, 2 * HEAD_DIM_C)
    else:
        cache_k, cache_v, layer, page_table = past

        def one_seq(args):
            q_s, k_s, v_s, pt = args
            n_past = pt.shape[0] * cache_k.shape[2]
            k_all = jnp.concatenate(
                [cache_k[layer, pt].reshape(n_past, N_HEADS_C, 2, HEAD_DIM_C), k_s], axis=0)
            v_all = jnp.concatenate(
                [cache_v[layer, pt].reshape(n_past, N_HEADS_C, 2 * HEAD_DIM_C), v_s], axis=0)
            o_s = diff_attn_core(q_s[None], k_all[None], v_all[None], n_past + jnp.arange(T),
                                 jnp.arange(n_past + T), rel_bias, lam)
            return o_s[0]

        o = lax.map(one_seq, (q, k, v, page_table))
    o = rmsnorm(o, subln_w, eps=1e-5) * (1.0 - lam_init)
    out = o.reshape(bsz, T, ATT_W) @ w_o
    return out, k.reshape(bsz, T, N_HEADS_C, 2 * HEAD_DIM_C), v


def swiglu(x, wg, wu, wd):
    return (jax.nn.silu(x @ wg) * (x @ wu)) @ wd


def trunk(x, conv0, ssm0, past, p):
    convs, ssms, vns, ks, vs = [], [], [], [], []
    h = x
    for i in range(DEPTH):
        j = i // 2
        hn = rmsnorm(h, p['norm_mix'][i])
        if i % 2 == 0:
            out, cb, hl, vn = ab_mixer(hn, conv0[j], ssm0[j], p['w_in_ab'][j], p['conv_w'][j], p['conv_b'][j],
                                       p['dt_bias'][j], p['a_log'][j], p['d_skip'][j], p['gnorm_w'][j],
                                       p['ln_v_g'][j], p['ln_v_b'][j], p['w_spatial'][j], p['b_spatial'][j],
                                       p['w_out_ab'][j])
            convs.append(cb)
            ssms.append(hl)
            vns.append(vn)
        else:
            layer_past = None if past is None else (past[0], past[1], j, past[2])
            out, kr, vr = diff_attention(hn, p['w_qkv'][j], p['lambda_q1'][j], p['lambda_k1'][j],
                                         p['lambda_q2'][j], p['lambda_k2'][j], p['subln_w'][j], p['w_o'][j],
                                         p['rel_bias'], lambda_init(i), layer_past)
            ks.append(kr)
            vs.append(vr)
        h = h + out
        h = h + swiglu(rmsnorm(h, p['norm_ffn'][i]), p['w_gate'][i], p['w_up'][i], p['w_down'][i])
    return rmsnorm(h, p['norm_final']), convs, ssms, vns, ks, vs


def setup_inputs(seed: int = 0) -> dict:
    key = jax.random.key(seed)
    k = jax.random.split(key, 32)
    f32 = jnp.float32

    def nrm(i, shape, scale):
        return jax.random.normal(k[i], shape, f32) * scale

    n_pages = PAST_LEN // PAGE_SIZE
    n_used = DEC_BATCH * n_pages
    n_pool = n_used + n_used // 4
    page_table = jax.random.permutation(k[0], n_pool)[:n_used].reshape(DEC_BATCH, n_pages).astype(jnp.int32)
    kv_shape = (N_ATT_LAYERS, n_pool, PAGE_SIZE, N_HEADS_C, 2 * HEAD_DIM_C)

    dt0 = jnp.exp(jax.random.uniform(k[1], (N_AB_LAYERS, N_HEADS_A), f32)
                  * (math.log(0.1) - math.log(0.001)) + math.log(0.001))
    dt_bias = dt0 + jnp.log(-jnp.expm1(-dt0))
    a_log = jnp.log(jax.random.uniform(k[2], (N_AB_LAYERS, N_HEADS_A), f32, 1.0, 16.0))

    return {
        'x_prompt': nrm(3, (BATCH, SEQ, D_MODEL), 1.0),
        'x_sample': nrm(4, (DEC_BATCH, DEC_SEQ, D_MODEL), 1.0),
        'cache_k': nrm(5, kv_shape, 1.0),
        'cache_v': nrm(6, kv_shape, 1.0),
        'page_table': page_table,
        'state_conv': nrm(7, (N_AB_LAYERS, DEC_BATCH, CONV_W - 1, CONV_DIM), 1.0),
        'state_ssm': nrm(8, (N_AB_LAYERS, DEC_BATCH, N_HEADS_A, HEADDIM_A, D_STATE), 0.5),
        'norm_mix': 1.0 + nrm(9, (DEPTH, D_MODEL), 0.02),
        'norm_ffn': 1.0 + nrm(10, (DEPTH, D_MODEL), 0.02),
        'norm_final': 1.0 + nrm(11, (D_MODEL,), 0.02),
        'w_in_ab': nrm(12, (N_AB_LAYERS, D_MODEL, IN_AB), D_MODEL ** -0.5),
        'conv_w': nrm(13, (N_AB_LAYERS, CONV_W, CONV_DIM), CONV_W ** -0.5),
        'conv_b': nrm(14, (N_AB_LAYERS, CONV_DIM), 0.01),
        'dt_bias': dt_bias,
        'a_log': a_log,
        'd_skip': 1.0 + nrm(15, (N_AB_LAYERS, N_HEADS_A), 0.01),
        'gnorm_w': 1.0 + nrm(16, (N_AB_LAYERS, D_INNER_A), 0.02),
        'ln_v_g': 1.0 + nrm(17, (N_AB_LAYERS, W_B), 0.02),
        'ln_v_b': nrm(18, (N_AB_LAYERS, W_B), 0.01),
        'w_spatial': nrm(19, (N_AB_LAYERS, N_GROUPS_B, CHUNK_B, CHUNK_B), CHUNK_B ** -0.5),
        'b_spatial': 1.0 + nrm(20, (N_AB_LAYERS, N_GROUPS_B, CHUNK_B), 0.01),
        'w_out_ab': nrm(21, (N_AB_LAYERS, MIX_AB, D_MODEL), MIX_AB ** -0.5),
        'w_qkv': nrm(22, (N_ATT_LAYERS, D_MODEL, 3 * ATT_W), D_MODEL ** -0.5),
        'lambda_q1': nrm(23, (N_ATT_LAYERS, HEAD_DIM_C), 0.1),
        'lambda_k1': nrm(24, (N_ATT_LAYERS, HEAD_DIM_C), 0.1),
        'lambda_q2': nrm(25, (N_ATT_LAYERS, HEAD_DIM_C), 0.1),
        'lambda_k2': nrm(26, (N_ATT_LAYERS, HEAD_DIM_C), 0.1),
        'subln_w': 1.0 + nrm(27, (N_ATT_LAYERS, 2 * HEAD_DIM_C), 0.02),
        'w_o': nrm(28, (N_ATT_LAYERS, ATT_W, D_MODEL), ATT_W ** -0.5),
        'rel_bias': nrm(29, (NUM_BUCKETS, N_HEADS_C), 0.5),
        'w_gate': nrm(30, (DEPTH, D_MODEL, D_FF), D_MODEL ** -0.5),
        'w_up': nrm(31, (DEPTH, D_MODEL, D_FF), D_MODEL ** -0.5),
        'w_down': jax.random.normal(jax.random.fold_in(key, 99), (DEPTH, D_FF, D_MODEL), f32) * D_FF ** -0.5,
    }


def reference(x_prompt, x_sample, cache_k, cache_v, page_table, state_conv, state_ssm,
              norm_mix, norm_ffn, norm_final, w_in_ab, conv_w, conv_b, dt_bias, a_log, d_skip,
              gnorm_w, ln_v_g, ln_v_b, w_spatial, b_spatial, w_out_ab, w_qkv, lambda_q1,
              lambda_k1, lambda_q2, lambda_k2, subln_w, w_o, rel_bias, w_gate, w_up, w_down):
    params = dict(norm_mix=norm_mix, norm_ffn=norm_ffn, norm_final=norm_final, w_in_ab=w_in_ab,
                  conv_w=conv_w, conv_b=conv_b, dt_bias=dt_bias, a_log=a_log, d_skip=d_skip,
                  gnorm_w=gnorm_w, ln_v_g=ln_v_g, ln_v_b=ln_v_b, w_spatial=w_spatial,
                  b_spatial=b_spatial, w_out_ab=w_out_ab, w_qkv=w_qkv, lambda_q1=lambda_q1,
                  lambda_k1=lambda_k1, lambda_q2=lambda_q2, lambda_k2=lambda_k2, subln_w=subln_w,
                  w_o=w_o, rel_bias=rel_bias, w_gate=w_gate, w_up=w_up, w_down=w_down)
    bp = x_prompt.shape[0]
    conv0 = jnp.zeros((N_AB_LAYERS, bp, CONV_W - 1, CONV_DIM), x_prompt.dtype)
    ssm0 = jnp.zeros((N_AB_LAYERS, bp, N_HEADS_A, HEADDIM_A, D_STATE), x_prompt.dtype)
    y_prompt, p_conv, p_ssm, _, p_k, p_v = trunk(x_prompt, conv0, ssm0, None, params)
    y_sample, s_conv, s_ssm, s_vn, s_k, s_v = trunk(x_sample, state_conv, state_ssm,
                                                    (cache_k, cache_v, page_table), params)
    return (y_prompt, y_sample, jnp.stack(p_conv), jnp.stack(p_ssm), jnp.stack(p_k), jnp.stack(p_v),
            jnp.stack(s_conv), jnp.stack(s_ssm), jnp.stack(s_vn), jnp.stack(s_k), jnp.stack(s_v))
```

```python
import functools
import math

import jax
import jax.numpy as jnp
from jax import lax
from jax.experimental import pallas as pl
from jax.experimental.pallas import tpu as pltpu

F32 = jnp.float32
BF16 = jnp.bfloat16

D_MODEL = 2048
N_PROMPT = 8192
N_SEQ_S = 32
T_S = 8
N_SAMPLE = N_SEQ_S * T_S
N_ROWS = N_PROMPT + N_SAMPLE
N_HEADS_A = 32
HEADDIM_A = 64
N_GROUPS_A = 4
D_STATE = 128
D_INNER = 2048
CONV_DIM = 3072
CONV_W = 4
CHUNK = 128
N_GROUPS_B = 8
W_B = 2048
N_HEADS_C = 8
HEAD_DIM_C = 128
ATT_W = 2048
PAGE = 128
NUM_BUCKETS = 32
MAX_DISTANCE = 128
D_FF = 5632
EPS = 1e-6
ATT_SCALE = HEAD_DIM_C ** -0.5

COL_Z = 0
COL_XBC = 2048
COL_U = 5120
COL_V = 7168
COL_DT = 9216
N_PROJ = 9728

VMEM_LIMIT_BYTES = 56 * 1024 * 1024
TM = 768
TN = 512
TQ = 256
TK = 256
PAGES_PER_STEP = 4
N_PAGE_STEPS = (N_PROMPT // PAGE) // PAGES_PER_STEP
NEG_INF = float("-inf")

_NT = (((1,), (1,)), ((), ()))
_TN = (((0,), (0,)), ((), ()))


def _params(n_grid):
    return pltpu.CompilerParams(dimension_semantics=("arbitrary",) * n_grid,
                                vmem_limit_bytes=VMEM_LIMIT_BYTES)


def _rep(a, n):
    return jnp.concatenate([a] * n, axis=1)


def _rms_to_scratch(x_ref, g_ref, xn_ref, eps):
    x = x_ref[...]
    ms = jnp.mean(x * x, axis=-1, keepdims=True)
    xn_ref[...] = (x * lax.rsqrt(ms + eps) * g_ref[...]).astype(BF16)


def _norm_mm_body(x_ref, g_ref, w_ref, *rest):
    o_refs, xn_ref = rest[:-1], rest[-1]

    @pl.when(pl.program_id(1) == 0)
    def _():
        _rms_to_scratch(x_ref, g_ref, xn_ref, EPS)

    r = jnp.dot(xn_ref[...], w_ref[...], preferred_element_type=F32)
    for o_ref in o_refs:
        o_ref[...] = r.astype(o_ref.dtype)


def _norm_matmul(x, g, w, out_dtypes):
    m, k = x.shape
    n = w.shape[1]
    return pl.pallas_call(
        _norm_mm_body,
        grid=(m // TM, n // TN),
        in_specs=[pl.BlockSpec((TM, k), lambda i, j: (i, 0)),
                  pl.BlockSpec((1, k), lambda i, j: (0, 0)),
                  pl.BlockSpec((k, TN), lambda i, j: (0, j))],
        out_specs=[pl.BlockSpec((TM, TN), lambda i, j: (i, j)) for _ in out_dtypes],
        out_shape=[jax.ShapeDtypeStruct((m, n), dt) for dt in out_dtypes],
        scratch_shapes=[pltpu.VMEM((TM, k), BF16)],
        compiler_params=_params(2),
        name="norm_matmul",
    )(x, g.reshape(1, k), w)


def _ffn_gate_body(x_ref, g_ref, wg_ref, wu_ref, o_ref, xn_ref):
    @pl.when(pl.program_id(1) == 0)
    def _():
        _rms_to_scratch(x_ref, g_ref, xn_ref, EPS)

    xn = xn_ref[...]
    a = jnp.dot(xn, wg_ref[...], preferred_element_type=F32)
    b = jnp.dot(xn, wu_ref[...], preferred_element_type=F32)
    o_ref[...] = (a * jax.nn.sigmoid(a) * b).astype(o_ref.dtype)


def _ffn_gate(x, g, wg, wu):
    m, k = x.shape
    n = wg.shape[1]
    return pl.pallas_call(
        _ffn_gate_body,
        grid=(m // TM, n // TN),
        in_specs=[pl.BlockSpec((TM, k), lambda i, j: (i, 0)),
                  pl.BlockSpec((1, k), lambda i, j: (0, 0)),
                  pl.BlockSpec((k, TN), lambda i, j: (0, j)),
                  pl.BlockSpec((k, TN), lambda i, j: (0, j))],
        out_specs=pl.BlockSpec((TM, TN), lambda i, j: (i, j)),
        out_shape=jax.ShapeDtypeStruct((m, n), BF16),
        scratch_shapes=[pltpu.VMEM((TM, k), BF16)],
        compiler_params=_params(2),
        name="ffn_gate",
    )(x, g.reshape(1, k), wg, wu)


def _mm_res_body(y_ref, w_ref, r_ref, o_ref):
    o_ref[...] = r_ref[...] + jnp.dot(y_ref[...], w_ref[...], preferred_element_type=F32)


def _matmul_residual(y, w, r):
    m, k = y.shape
    n = w.shape[1]
    return pl.pallas_call(
        _mm_res_body,
        grid=(m // TM, n // TN),
        in_specs=[pl.BlockSpec((TM, k), lambda i, j: (i, 0)),
                  pl.BlockSpec((k, TN), lambda i, j: (0, j)),
                  pl.BlockSpec((TM, TN), lambda i, j: (i, j))],
        out_specs=pl.BlockSpec((TM, TN), lambda i, j: (i, j)),
        out_shape=jax.ShapeDtypeStruct((m, n), F32),
        compiler_params=_params(2),
        name="matmul_residual",
    )(y, w, r)


def _rmsnorm_body(x_ref, g_ref, o_ref):
    x = x_ref[...]
    ms = jnp.mean(x * x, axis=-1, keepdims=True)
    o_ref[...] = x * lax.rsqrt(ms + EPS) * g_ref[...]


def _rmsnorm(x, g):
    m, k = x.shape
    return pl.pallas_call(
        _rmsnorm_body,
        grid=(m // TM,),
        in_specs=[pl.BlockSpec((TM, k), lambda i: (i, 0)),
                  pl.BlockSpec((1, k), lambda i: (0, 0))],
        out_specs=pl.BlockSpec((TM, k), lambda i: (i, 0)),
        out_shape=jax.ShapeDtypeStruct((m, k), F32),
        compiler_params=_params(1),
        name="final_rmsnorm",
    )(x, g.reshape(1, k))


def _softplus(x):
    return jnp.maximum(x, 0.0) + jnp.log1p(jnp.exp(-jnp.abs(x)))


def _gelu_exact(x):
    return 0.5 * x * (1.0 + lax.erf(x * math.sqrt(0.5)))


def _pair_cols(a, h0, lo):
    return jnp.where(lo, a[:, h0:h0 + 1], a[:, h0 + 1:h0 + 2])


def _ab_body(x_ref, conv0_ref, ssm0_ref, convw_ref, convb_ref, dtb_ref, alog_ref, dskip_ref, gnw_ref,
             lng_ref, lnb_ref, ws_ref, bst_ref, *rest, lr, emit_vn):
    if emit_vn:
        y_ref, ssm_out_ref, vn_ref, xp_ref, h_ref = rest
    else:
        y_ref, ssm_out_ref, xp_ref, h_ref = rest
        vn_ref = None
    c = pl.program_id(1)

    @pl.when(c == 0)
    def _():
        xp_ref[0:8, :] = conv0_ref[0]
        h_ref[...] = ssm0_ref[0]

    rows = x_ref[...]
    if lr < CHUNK:
        rows = jnp.concatenate([rows, jnp.zeros((CHUNK - lr, N_PROJ), F32)], axis=0)
    z = rows[:, COL_Z:COL_Z + D_INNER]
    u = rows[:, COL_U:COL_U + W_B]
    v = rows[:, COL_V:COL_V + W_B]
    dt_raw = rows[:, COL_DT:COL_DT + 128]

    xp_ref[8:8 + CHUNK, :] = rows[:, COL_XBC:COL_XBC + CONV_DIM]
    conv = convb_ref[...]
    for tap in range(CONV_W):
        conv = conv + convw_ref[tap:tap + 1, :] * xp_ref[5 + tap:5 + tap + CHUNK, :]
    xp_ref[0:8, :] = xp_ref[lr:lr + 8, :]
    xbc = conv * jax.nn.sigmoid(conv)
    xs = xbc[:, :D_INNER]
    bm = xbc[:, D_INNER:D_INNER + N_GROUPS_A * D_STATE]
    cm = xbc[:, D_INNER + N_GROUPS_A * D_STATE:]

    row_i = lax.broadcasted_iota(jnp.int32, (CHUNK, CHUNK), 0)
    col_i = lax.broadcasted_iota(jnp.int32, (CHUNK, CHUNK), 1)
    causal = row_i >= col_i
    lo = col_i < HEADDIM_A
    head_lane = lax.broadcasted_iota(jnp.int32, (1, CHUNK), 1) < N_HEADS_A

    a_neg = jnp.where(head_lane, -jnp.exp(alog_ref[...]), 0.0)
    dtv = _softplus(dt_raw + dtb_ref[...])
    if lr < CHUNK:
        dtv = jnp.where(row_i < lr, dtv, 0.0)
    a_cum = jnp.dot(causal.astype(F32), dtv * a_neg, preferred_element_type=F32,
                    precision=lax.Precision.HIGHEST)
    a_cum_t = a_cum.T
    a_last = a_cum[CHUNK - 1:CHUNK, :]
    decay_end = jnp.exp(a_last - a_cum)
    decay_in = jnp.exp(a_cum)
    chunk_decay = jnp.exp(a_last)

    y_parts = []
    for g in range(N_GROUPS_A):
        bg = bm[:, g * D_STATE:(g + 1) * D_STATE].astype(BF16)
        cg = cm[:, g * D_STATE:(g + 1) * D_STATE].astype(BF16)
        cb = lax.dot_general(cg, bg, _NT, preferred_element_type=F32)
        rows_g = slice(g * 8 * HEADDIM_A, (g + 1) * 8 * HEADDIM_A)
        y_off = lax.dot_general(cg, h_ref[rows_g, :].astype(BF16), _NT,
                                preferred_element_type=F32)
        xdtd_parts = []
        for jj in range(4):
            j = g * 4 + jj
            h0 = 2 * j
            xdt = xs[:, j * 128:(j + 1) * 128] * _pair_cols(dtv, h0, lo)
            m_pair = []
            for h in (h0, h0 + 1):
                seg = a_cum[:, h:h + 1] - a_cum_t[h:h + 1, :]
                m_pair.append((cb * jnp.exp(jnp.where(causal, seg, NEG_INF))).astype(BF16))
            lhs = jnp.concatenate(m_pair, axis=1)
            rhs = jnp.concatenate([jnp.where(lo, xdt, 0.0).astype(BF16),
                                   jnp.where(lo, 0.0, xdt).astype(BF16)], axis=0)
            y_diag = jnp.dot(lhs, rhs, preferred_element_type=F32)
            y_parts.append(y_diag + y_off[:, jj * 128:(jj + 1) * 128] * _pair_cols(decay_in, h0, lo))
            xdtd_parts.append((xdt * _pair_cols(decay_end, h0, lo)).astype(BF16))
        xdtd = jnp.concatenate(xdtd_parts, axis=1)
        st = lax.dot_general(xdtd, bg, _TN, preferred_element_type=F32)
        for e in range(8):
            h = g * 8 + e
            r0 = h * HEADDIM_A
            h_ref[r0:r0 + HEADDIM_A, :] = (h_ref[r0:r0 + HEADDIM_A, :] * chunk_decay[0:1, h:h + 1]
                                           + st[e * HEADDIM_A:(e + 1) * HEADDIM_A, :])

    @pl.when(c == pl.num_programs(1) - 1)
    def _():
        ssm_out_ref[0] = h_ref[...]

    y = jnp.concatenate(y_parts, axis=1) + dskip_ref[...] * xs
    gated = y * (z * jax.nn.sigmoid(z))
    ya_parts = []
    gw = D_INNER // N_GROUPS_A
    for g in range(N_GROUPS_A):
        gg = gated[:, g * gw:(g + 1) * gw]
        ms = jnp.mean(gg * gg, axis=-1, keepdims=True)
        ya_parts.append(gg * lax.rsqrt(ms + 1e-5) * gnw_ref[:, g * gw:(g + 1) * gw])

    ug = _gelu_exact(u)
    vg = _gelu_exact(v)
    mu = jnp.mean(vg, axis=-1, keepdims=True)
    xc = vg - mu
    var = jnp.mean(xc * xc, axis=-1, keepdims=True)
    vn = xc * lax.rsqrt(var + 1e-5) * lng_ref[...] + lnb_ref[...]
    vn16 = vn.astype(BF16)
    wb = W_B // N_GROUPS_B
    yb_parts = []
    for g in range(N_GROUPS_B):
        w_causal = jnp.where(causal, ws_ref[g], 0.0).astype(BF16)
        mix = jnp.dot(w_causal, vn16[:, g * wb:(g + 1) * wb], preferred_element_type=F32)
        yb_parts.append(ug[:, g * wb:(g + 1) * wb] * (mix + bst_ref[:, g:g + 1]))

    out = jnp.concatenate(ya_parts + yb_parts, axis=1)
    y_ref[...] = out[:lr].astype(y_ref.dtype)
    if emit_vn:
        vn_ref[...] = vn[:lr]


def _ab_mixer(proj, conv0, ssm0, p, *, n_seq, n_chunk, lr, row_block0, out_dtype, emit_vn):
    n_out_rows = n_seq * n_chunk * lr
    row_map = lambda s, c: (row_block0 + s * n_chunk + c, 0)
    out_map = lambda s, c: (s * n_chunk + c, 0)
    const2 = lambda s, c: (0, 0)
    out_specs = [pl.BlockSpec((lr, 2 * D_INNER), out_map),
                 pl.BlockSpec((1, N_HEADS_A * HEADDIM_A, D_STATE), lambda s, c: (s, 0, 0))]
    out_shape = [jax.ShapeDtypeStruct((n_out_rows, 2 * D_INNER), out_dtype),
                 jax.ShapeDtypeStruct((n_seq, N_HEADS_A * HEADDIM_A, D_STATE), F32)]
    if emit_vn:
        out_specs.append(pl.BlockSpec((lr, W_B), out_map))
        out_shape.append(jax.ShapeDtypeStruct((n_out_rows, W_B), F32))
    return pl.pallas_call(
        functools.partial(_ab_body, lr=lr, emit_vn=emit_vn),
        grid=(n_seq, n_chunk),
        in_specs=[pl.BlockSpec((lr, N_PROJ), row_map),
                  pl.BlockSpec((1, 8, CONV_DIM), lambda s, c: (s, 0, 0)),
                  pl.BlockSpec((1, N_HEADS_A * HEADDIM_A, D_STATE), lambda s, c: (s, 0, 0)),
                  pl.BlockSpec((CONV_W, CONV_DIM), const2),
                  pl.BlockSpec((1, CONV_DIM), const2),
                  pl.BlockSpec((1, 128), const2),
                  pl.BlockSpec((1, 128), const2),
                  pl.BlockSpec((1, D_INNER), const2),
                  pl.BlockSpec((1, D_INNER), const2),
                  pl.BlockSpec((1, W_B), const2),
                  pl.BlockSpec((1, W_B), const2),
                  pl.BlockSpec((N_GROUPS_B, CHUNK, CHUNK), lambda s, c: (0, 0, 0)),
                  pl.BlockSpec((CHUNK, N_GROUPS_B), const2)],
        out_specs=out_specs,
        out_shape=out_shape,
        scratch_shapes=[pltpu.VMEM((8 + CHUNK, CONV_DIM), F32),
                        pltpu.VMEM((N_HEADS_A * HEADDIM_A, D_STATE), F32)],
        compiler_params=_params(2),
        name="ab_mixer",
    )(proj, conv0, ssm0, p["conv_w"], p["conv_b"], p["dt_bias"], p["a_log"], p["d_skip"], p["gnorm_w"],
      p["ln_v_g"], p["ln_v_b"], p["w_spatial"], p["b_spatial_t"])


def _t5_bucket(rel):
    n = jnp.maximum(rel, 0)
    max_exact = NUM_BUCKETS // 2
    nf = jnp.maximum(n, 1).astype(F32)
    large = max_exact + (jnp.log(nf / max_exact) / math.log(MAX_DISTANCE / max_exact)
                         * (NUM_BUCKETS - max_exact)).astype(jnp.int32)
    large = jnp.minimum(large, NUM_BUCKETS - 1)
    return jnp.where(n < max_exact, n, large)


def _bias_body(tab_ref, pt_ref, tail_ref, own_ref, far_ref):
    h = pl.program_id(0)

    def lookup(rel):
        bucket = _t5_bucket(rel)
        val = jnp.zeros(rel.shape, F32)
        for b in range(NUM_BUCKETS):
            val = jnp.where(bucket == b, tab_ref[b, h], val)
        return jnp.where(rel >= 0, val, NEG_INF)

    r = lax.broadcasted_iota(jnp.int32, (TQ, TK), 0)
    c = lax.broadcasted_iota(jnp.int32, (TQ, TK), 1)
    for d in range(2):
        pt_ref[0, d] = lookup(d * TK + r - c)

    n_tail = PAGES_PER_STEP * PAGE
    t_tail = lax.broadcasted_iota(jnp.int32, (2 * T_S, n_tail), 0) & (T_S - 1)
    c_tail = lax.broadcasted_iota(jnp.int32, (2 * T_S, n_tail), 1)
    tail_ref[...] = lookup(n_tail + t_tail - c_tail)
    t_own = lax.broadcasted_iota(jnp.int32, (2 * T_S, 128), 0) & (T_S - 1)
    c_own = lax.broadcasted_iota(jnp.int32, (2 * T_S, 128), 1)
    own_ref[...] = jnp.where(c_own < T_S, lookup(t_own - c_own), NEG_INF)
    far_ref[...] = jnp.full((2 * T_S, 128), tab_ref[NUM_BUCKETS - 1, h], F32)


def _bias_tiles(rel_bias):
    n_tail = PAGES_PER_STEP * PAGE
    rows = lambda n: pl.BlockSpec((2 * T_S, n), lambda h: (h, 0))
    return pl.pallas_call(
        _bias_body,
        grid=(N_HEADS_C,),
        in_specs=[pl.BlockSpec(memory_space=pltpu.SMEM)],
        out_specs=[pl.BlockSpec((1, 2, TQ, TK), lambda h: (h, 0, 0, 0)), rows(n_tail), rows(128), rows(128)],
        out_shape=[jax.ShapeDtypeStruct((N_HEADS_C, 2, TQ, TK), F32),
                   jax.ShapeDtypeStruct((128, n_tail), F32),
                   jax.ShapeDtypeStruct((128, 128), F32),
                   jax.ShapeDtypeStruct((128, 128), F32)],
        compiler_params=_params(1),
        name="t5_bias_tiles",
    )(rel_bias)


def _lambda(lq1_ref, lk1_ref, lq2_ref, lk2_ref, lam_init):
    s1 = jnp.sum(lq1_ref[...] * lk1_ref[...], axis=-1, keepdims=True)
    s2 = jnp.sum(lq2_ref[...] * lk2_ref[...], axis=-1, keepdims=True)
    return jnp.exp(s1) - jnp.exp(s2) + lam_init


def _subln(o, w, lam_init):
    ms = jnp.mean(o * o, axis=-1, keepdims=True)
    return o * lax.rsqrt(ms + 1e-5) * w * (1.0 - lam_init)


def _softmax_step(s, v16, m_ref, l_ref, acc_ref, idx, n_rep_s, n_rep_acc):
    m_old = m_ref[idx]
    m_new = jnp.maximum(m_old, jnp.max(s, axis=-1, keepdims=True))
    alpha = jnp.exp(m_old - m_new)
    p = jnp.exp(s - _rep(m_new, n_rep_s))
    l_ref[idx] = alpha * l_ref[idx] + jnp.sum(p, axis=-1, keepdims=True)
    acc_ref[idx] = acc_ref[idx] * _rep(alpha, n_rep_acc) + jnp.dot(p.astype(BF16), v16,
                                                                   preferred_element_type=F32)
    m_ref[idx] = m_new


def _pattn_body(q_ref, k_ref, v_ref, bias_ref, tab_ref, lq1_ref, lk1_ref, lq2_ref, lk2_ref, subw_ref,
                o_ref, m_ref, l_ref, acc_ref, *, lam_init):
    h = pl.program_id(0)
    qi = pl.program_id(1)
    q = q_ref[...]
    qs = (q[:, :HEAD_DIM_C], q[:, HEAD_DIM_C:])
    m_ref[...] = jnp.full(m_ref.shape, NEG_INF, F32)
    l_ref[...] = jnp.zeros(l_ref.shape, F32)
    acc_ref[...] = jnp.zeros(acc_ref.shape, F32)
    far_bias = tab_ref[NUM_BUCKETS - 1, h]

    def step(j, bias):
        start = pl.multiple_of(j * TK, TK)
        k = k_ref[pl.ds(start, TK), :]
        v = v_ref[pl.ds(start, TK), :]
        for c in range(2):
            s = lax.dot_general(qs[c], k[:, c * HEAD_DIM_C:(c + 1) * HEAD_DIM_C], _NT,
                                preferred_element_type=F32) * ATT_SCALE + bias
            _softmax_step(s, v, m_ref, l_ref, acc_ref, c, TK // 128, 2 * HEAD_DIM_C // 128)

    def far_step(j, carry):
        step(j, far_bias)
        return carry

    lax.fori_loop(0, jnp.maximum(qi - 1, 0), far_step, 0)

    @pl.when(qi >= 1)
    def _():
        step(qi - 1, bias_ref[0, 1])

    step(qi, bias_ref[0, 0])

    lam = _lambda(lq1_ref, lk1_ref, lq2_ref, lk2_ref, lam_init)
    n_rep = 2 * HEAD_DIM_C // 128
    o = (acc_ref[0] * _rep(1.0 / l_ref[0], n_rep)
         - lam * (acc_ref[1] * _rep(1.0 / l_ref[1], n_rep)))
    o_ref[...] = _subln(o, subw_ref[...], lam_init).astype(o_ref.dtype)


def _prompt_attention(qkv16, bias_pt, tab_t, p, lam_init):
    dh2 = 2 * HEAD_DIM_C
    vec = lambda n: pl.BlockSpec((1, n), lambda h, i: (0, 0))
    return pl.pallas_call(
        functools.partial(_pattn_body, lam_init=lam_init),
        grid=(N_HEADS_C, N_PROMPT // TQ),
        in_specs=[pl.BlockSpec((TQ, dh2), lambda h, i: (i, h)),
                  pl.BlockSpec((N_PROMPT, dh2), lambda h, i: (0, N_HEADS_C + h)),
                  pl.BlockSpec((N_PROMPT, dh2), lambda h, i: (0, 2 * N_HEADS_C + h)),
                  pl.BlockSpec((1, 2, TQ, TK), lambda h, i: (h, 0, 0, 0)),
                  pl.BlockSpec(memory_space=pltpu.SMEM),
                  vec(HEAD_DIM_C), vec(HEAD_DIM_C), vec(HEAD_DIM_C), vec(HEAD_DIM_C), vec(dh2)],
        out_specs=pl.BlockSpec((TQ, dh2), lambda h, i: (i, h)),
        out_shape=jax.ShapeDtypeStruct((N_PROMPT, ATT_W), BF16),
        scratch_shapes=[pltpu.VMEM((2, TQ, 128), F32),
                        pltpu.VMEM((2, TQ, 128), F32),
                        pltpu.VMEM((2, TQ, dh2), F32)],
        compiler_params=_params(2),
        name="prompt_attention",
    )(qkv16, qkv16, qkv16, bias_pt, tab_t, p["lq1"], p["lk1"], p["lq2"], p["lk2"], p["subln_w"])


def _sattn_body(pt_ref, q_ref, ko_ref, vo_ref, *rest, lam_init):
    k_refs = rest[:PAGES_PER_STEP]
    v_refs = rest[PAGES_PER_STEP:2 * PAGES_PER_STEP]
    (tail_ref, own_ref, far_ref, lq1_ref, lk1_ref, lq2_ref, lk2_ref, subw_ref,
     o_ref, qb_ref, m_ref, l_ref, acc_ref) = rest[2 * PAGES_PER_STEP:]
    del pt_ref
    j = pl.program_id(1)
    n_feat = ATT_W

    @pl.when(j == 0)
    def _():
        qt = jnp.concatenate([q_ref[...]] * 16, axis=0)
        rg = lax.shift_right_logical(lax.broadcasted_iota(jnp.int32, (128, n_feat), 0), 3)
        cg = lax.shift_right_logical(lax.broadcasted_iota(jnp.int32, (128, n_feat), 1), 7)
        qb_ref[...] = jnp.where(rg == cg, qt, 0.0).astype(BF16)
        m_ref[...] = jnp.full(m_ref.shape, NEG_INF, F32)
        l_ref[...] = jnp.zeros(l_ref.shape, F32)
        acc_ref[...] = jnp.zeros(acc_ref.shape, F32)

    def step(k16, v16, bias):
        s = lax.dot_general(qb_ref[...], k16, _NT, preferred_element_type=F32) * ATT_SCALE + bias
        _softmax_step(s, v16, m_ref, l_ref, acc_ref, 0, s.shape[1] // 128, n_feat // 128)

    def page_rows(ref):
        return jnp.concatenate([ref[0, :, h, :] for h in range(N_HEADS_C)], axis=1)

    is_last = j == pl.num_programs(1) - 1
    for pair in range(PAGES_PER_STEP // 2):
        k16 = jnp.concatenate([page_rows(k_refs[2 * pair]), page_rows(k_refs[2 * pair + 1])],
                              axis=0).astype(BF16)
        v16 = jnp.concatenate([page_rows(v_refs[2 * pair]), page_rows(v_refs[2 * pair + 1])],
                              axis=0).astype(BF16)
        bias = jnp.where(is_last, tail_ref[:, pair * 2 * PAGE:(pair + 1) * 2 * PAGE], _rep(far_ref[...], 2))
        step(k16, v16, bias)

    @pl.when(is_last)
    def _():
        pad = jnp.zeros((128 - T_S, n_feat), F32)
        step(jnp.concatenate([ko_ref[...], pad], axis=0).astype(BF16),
             jnp.concatenate([vo_ref[...], pad], axis=0).astype(BF16), own_ref[...])
        lam = _lambda(lq1_ref, lk1_ref, lq2_ref, lk2_ref, lam_init)
        inv_l = 1.0 / l_ref[0]
        dh2 = 2 * HEAD_DIM_C
        for h in range(N_HEADS_C):
            r0 = h * 2 * T_S
            cols = slice(h * dh2, (h + 1) * dh2)
            o0 = acc_ref[0, r0:r0 + T_S, cols] * _rep(inv_l[r0:r0 + T_S], dh2 // 128)
            o1 = acc_ref[0, r0 + T_S:r0 + 2 * T_S, cols] * _rep(inv_l[r0 + T_S:r0 + 2 * T_S], dh2 // 128)
            o_ref[0, :, cols] = _subln(o0 - lam * o1, subw_ref[...], lam_init)


def _sample_attention(page_table, qkv, cache_k, cache_v, tail, own, far, p, lam_init):
    row0 = N_PROMPT // T_S
    own_spec = lambda col: pl.BlockSpec((T_S, ATT_W), lambda b, j, pt: (row0 + b, col))
    page_spec = lambda i: pl.BlockSpec((1, PAGE, N_HEADS_C, 2 * HEAD_DIM_C),
                                       lambda b, j, pt: (pt[b, j * PAGES_PER_STEP + i], 0, 0, 0))
    const = lambda shape: pl.BlockSpec(shape, lambda b, j, pt: (0, 0))
    grid_spec = pltpu.PrefetchScalarGridSpec(
        num_scalar_prefetch=1,
        grid=(N_SEQ_S, N_PAGE_STEPS),
        in_specs=([own_spec(0), own_spec(1), own_spec(2)]
                  + [page_spec(i) for i in range(PAGES_PER_STEP)]
                  + [page_spec(i) for i in range(PAGES_PER_STEP)]
                  + [const((128, PAGES_PER_STEP * PAGE)), const((128, 128)), const((128, 128)),
                     const((1, HEAD_DIM_C)), const((1, HEAD_DIM_C)), const((1, HEAD_DIM_C)),
                     const((1, HEAD_DIM_C)), const((1, 2 * HEAD_DIM_C))]),
        out_specs=pl.BlockSpec((1, T_S, ATT_W), lambda b, j, pt: (b, 0, 0)),
        scratch_shapes=[pltpu.VMEM((128, ATT_W), BF16),
                        pltpu.VMEM((1, 128, 128), F32),
                        pltpu.VMEM((1, 128, 128), F32),
                        pltpu.VMEM((1, 128, ATT_W), F32)],
    )
    return pl.pallas_call(
        functools.partial(_sattn_body, lam_init=lam_init),
        grid_spec=grid_spec,
        out_shape=jax.ShapeDtypeStruct((N_SEQ_S, T_S, ATT_W), F32),
        compiler_params=_params(2),
        name="sample_attention",
    )(page_table, qkv, qkv, qkv, *([cache_k] * PAGES_PER_STEP), *([cache_v] * PAGES_PER_STEP),
      tail, own, far, p["lq1"], p["lk1"], p["lq2"], p["lk2"], p["subln_w"])


def kernel(x_prompt, x_sample, cache_k, cache_v, page_table, state_conv, state_ssm, norm_mix, norm_ffn, norm_final, w_in_ab, conv_w, conv_b, dt_bias, a_log, d_skip, gnorm_w, ln_v_g, ln_v_b, w_spatial, b_spatial, w_out_ab, w_qkv, lambda_q1, lambda_k1, lambda_q2, lambda_k2, subln_w, w_o, rel_bias, w_gate, w_up, w_down):
    x = jnp.concatenate([x_prompt.reshape(N_PROMPT, D_MODEL), x_sample.reshape(N_SAMPLE, D_MODEL)], axis=0)

    w_in = w_in_ab[0]
    n_zx = D_INNER + CONV_DIM
    w_cat = jnp.concatenate(
        [w_in[:, :n_zx], w_in[:, n_zx + N_HEADS_A:], w_in[:, n_zx:n_zx + N_HEADS_A],
         jnp.zeros((D_MODEL, N_PROJ - COL_DT - N_HEADS_A), F32)], axis=1).astype(BF16)
    (proj,) = _norm_matmul(x, norm_mix[0], w_cat, [F32])

    pad_heads = lambda a: jnp.pad(a.reshape(1, N_HEADS_A), ((0, 0), (0, 128 - N_HEADS_A)))
    ab_params = dict(
        conv_w=conv_w[0], conv_b=conv_b[0].reshape(1, CONV_DIM),
        dt_bias=pad_heads(dt_bias[0]), a_log=pad_heads(a_log[0]),
        d_skip=jnp.repeat(d_skip[0], HEADDIM_A).reshape(1, D_INNER),
        gnorm_w=gnorm_w[0].reshape(1, D_INNER),
        ln_v_g=ln_v_g[0].reshape(1, W_B), ln_v_b=ln_v_b[0].reshape(1, W_B),
        w_spatial=w_spatial[0], b_spatial_t=b_spatial[0].T)
    n_state_rows = N_HEADS_A * HEADDIM_A
    y_p, ssm_p = _ab_mixer(
        proj, jnp.zeros((1, 8, CONV_DIM), F32), jnp.zeros((1, n_state_rows, D_STATE), F32), ab_params,
        n_seq=1, n_chunk=N_PROMPT // CHUNK, lr=CHUNK, row_block0=0, out_dtype=BF16, emit_vn=False)
    conv0_s = jnp.pad(state_conv[0], ((0, 0), (8 - (CONV_W - 1), 0), (0, 0)))
    y_s, ssm_s, vn_s = _ab_mixer(
        proj, conv0_s, state_ssm[0].reshape(N_SEQ_S, n_state_rows, D_STATE), ab_params,
        n_seq=N_SEQ_S, n_chunk=1, lr=T_S, row_block0=N_PROMPT // T_S, out_dtype=F32, emit_vn=True)
    y_ab = jnp.concatenate([y_p, y_s.astype(BF16)], axis=0)
    h = _matmul_residual(y_ab, w_out_ab[0].astype(BF16), x)
    h = _matmul_residual(_ffn_gate(h, norm_ffn[0], w_gate[0].astype(BF16), w_up[0].astype(BF16)),
                         w_down[0].astype(BF16), h)

    lam_init = 0.8 - 0.6 * math.exp(-0.3 * 1)
    qkv, qkv16 = _norm_matmul(h, norm_mix[1], w_qkv[0].astype(BF16), [F32, BF16])
    att_params = dict(lq1=lambda_q1[0].reshape(1, HEAD_DIM_C), lk1=lambda_k1[0].reshape(1, HEAD_DIM_C),
                      lq2=lambda_q2[0].reshape(1, HEAD_DIM_C), lk2=lambda_k2[0].reshape(1, HEAD_DIM_C),
                      subln_w=subln_w[0].reshape(1, 2 * HEAD_DIM_C))
    bias_pt, bias_tail, bias_own, bias_far = _bias_tiles(rel_bias)
    o_p = _prompt_attention(qkv16, bias_pt, rel_bias, att_params, lam_init)
    n_pool = cache_k.shape[1]
    page_shape = (n_pool, PAGE, N_HEADS_C, 2 * HEAD_DIM_C)
    o_s = _sample_attention(page_table, qkv, cache_k[0].reshape(page_shape), cache_v[0].reshape(page_shape),
                            bias_tail, bias_own, bias_far, att_params, lam_init)
    o = jnp.concatenate([o_p, o_s.reshape(N_SAMPLE, ATT_W).astype(BF16)], axis=0)
    h = _matmul_residual(o, w_o[0].astype(BF16), h)
    h = _matmul_residual(_ffn_gate(h, norm_ffn[1], w_gate[1].astype(BF16), w_up[1].astype(BF16)),
                         w_down[1].astype(BF16), h)
    y = _rmsnorm(h, norm_final)

    proj_s = proj[N_PROMPT:].reshape(N_SEQ_S, T_S, N_PROJ)
    xbc_cols = slice(COL_XBC, COL_XBC + CONV_DIM)
    kv_shape_p = (1, 1, N_PROMPT, N_HEADS_C, 2 * HEAD_DIM_C)
    kv_shape_s = (1, N_SEQ_S, T_S, N_HEADS_C, 2 * HEAD_DIM_C)
    return (
        y[:N_PROMPT].reshape(1, N_PROMPT, D_MODEL),
        y[N_PROMPT:].reshape(N_SEQ_S, T_S, D_MODEL),
        proj[N_PROMPT - (CONV_W - 1):N_PROMPT, xbc_cols].reshape(1, 1, CONV_W - 1, CONV_DIM),
        ssm_p.reshape(1, 1, N_HEADS_A, HEADDIM_A, D_STATE),
        qkv[:N_PROMPT, ATT_W:2 * ATT_W].reshape(kv_shape_p),
        qkv[:N_PROMPT, 2 * ATT_W:].reshape(kv_shape_p),
        proj_s[:, T_S - (CONV_W - 1):, xbc_cols].reshape(1, N_SEQ_S, CONV_W - 1, CONV_DIM),
        ssm_s.reshape(1, N_SEQ_S, N_HEADS_A, HEADDIM_A, D_STATE),
        vn_s.reshape(1, N_SEQ_S, T_S, W_B),
        qkv[N_PROMPT:, ATT_W:2 * ATT_W].reshape(kv_shape_s),
        qkv[N_PROMPT:, 2 * ATT_W:].reshape(kv_shape_s),
    )
```

```python
import functools
import math

import jax
import jax.numpy as jnp
from jax import lax
from jax.experimental import pallas as pl
from jax.experimental.pallas import tpu as pltpu

F32 = jnp.float32
BF16 = jnp.bfloat16

D_MODEL = 2048
N_PROMPT = 8192
N_SEQ_S = 32
T_S = 8
N_SAMPLE = N_SEQ_S * T_S
N_ROWS = N_PROMPT + N_SAMPLE
N_HEADS_A = 32
HEADDIM_A = 64
N_GROUPS_A = 4
D_STATE = 128
D_INNER = 2048
CONV_DIM = 3072
CONV_W = 4
CHUNK = 128
N_GROUPS_B = 8
W_B = 2048
N_HEADS_C = 8
HEAD_DIM_C = 128
ATT_W = 2048
PAGE = 128
NUM_BUCKETS = 32
MAX_DISTANCE = 128
D_FF = 5632
EPS = 1e-6
ATT_SCALE = HEAD_DIM_C ** -0.5

COL_Z = 0
COL_XBC = 2048
COL_U = 5120
COL_V = 7168
COL_DT = 9216
N_PROJ = 9728

VMEM_LIMIT_BYTES = 56 * 1024 * 1024
TM = 768
TN = 512
TQ = 256
TK = 256
PAGES_PER_STEP = 8
N_PAGE_STEPS = (N_PROMPT // PAGE) // PAGES_PER_STEP
PAGE_COLS = PAGE * N_HEADS_C
LOG2E = math.log2(math.e)
SCORE_SCALE2 = ATT_SCALE * LOG2E
NEG_INF = float("-inf")

_NT = (((1,), (1,)), ((), ()))
_TN = (((0,), (0,)), ((), ()))


def _params(n_grid):
    return pltpu.CompilerParams(dimension_semantics=("arbitrary",) * n_grid,
                                vmem_limit_bytes=VMEM_LIMIT_BYTES)


def _rep(a, n):
    return jnp.concatenate([a] * n, axis=1)


def _rms_to_scratch(x_ref, g_ref, xn_ref, eps):
    x = x_ref[...]
    ms = jnp.mean(x * x, axis=-1, keepdims=True)
    xn_ref[...] = (x * lax.rsqrt(ms + eps) * g_ref[...]).astype(BF16)


def _norm_mm_body(x_ref, g_ref, *rest, starts):
    n_parts = len(starts) - 1
    w_refs, o_refs, xn_ref = rest[:n_parts], rest[n_parts:-1], rest[-1]
    j = pl.program_id(1)

    @pl.when(j == 0)
    def _():
        _rms_to_scratch(x_ref, g_ref, xn_ref, EPS)

    def emit(w_ref):
        r = jnp.dot(xn_ref[...], w_ref[...], preferred_element_type=F32)
        for o_ref in o_refs:
            o_ref[...] = r.astype(o_ref.dtype)

    if n_parts == 1:
        emit(w_refs[0])
    else:
        for p, w_ref in enumerate(w_refs):
            pl.when((j >= starts[p]) & (j < starts[p + 1]))(functools.partial(emit, w_ref))


def _norm_matmul(x, g, parts, out_dtypes):
    m, k = x.shape
    starts = [0]
    for _, _, _, n_tiles in parts:
        starts.append(starts[-1] + n_tiles)

    def w_spec(p):
        _, layer, tile0, n_tiles = parts[p]
        lo = starts[p]
        return pl.BlockSpec((None, k, TN), lambda i, j: (layer, 0, tile0 + jnp.clip(j - lo, 0, n_tiles - 1)))

    n = starts[-1] * TN
    return pl.pallas_call(
        functools.partial(_norm_mm_body, starts=tuple(starts)),
        grid=(m // TM, starts[-1]),
        in_specs=[pl.BlockSpec((TM, k), lambda i, j: (i, 0)),
                  pl.BlockSpec((1, k), lambda i, j: (0, 0))] + [w_spec(p) for p in range(len(parts))],
        out_specs=[pl.BlockSpec((TM, TN), lambda i, j: (i, j)) for _ in out_dtypes],
        out_shape=[jax.ShapeDtypeStruct((m, n), dt) for dt in out_dtypes],
        scratch_shapes=[pltpu.VMEM((TM, k), BF16)],
        compiler_params=_params(2),
        name="norm_matmul",
    )(x, g.reshape(1, k), *[w for w, _, _, _ in parts])


def _ffn_gate_body(x_ref, g_ref, wg_ref, wu_ref, o_ref, xn_ref):
    @pl.when(pl.program_id(1) == 0)
    def _():
        _rms_to_scratch(x_ref, g_ref, xn_ref, EPS)

    xn = xn_ref[...]
    a = jnp.dot(xn, wg_ref[...], preferred_element_type=F32)
    b = jnp.dot(xn, wu_ref[...], preferred_element_type=F32)
    o_ref[...] = (a * jax.nn.sigmoid(a) * b).astype(o_ref.dtype)


def _ffn_gate(x, g, wg, wu, layer):
    m, k = x.shape
    n = wg.shape[2]
    return pl.pallas_call(
        _ffn_gate_body,
        grid=(m // TM, n // TN),
        in_specs=[pl.BlockSpec((TM, k), lambda i, j: (i, 0)),
                  pl.BlockSpec((1, k), lambda i, j: (0, 0)),
                  pl.BlockSpec((None, k, TN), lambda i, j: (layer, 0, j)),
                  pl.BlockSpec((None, k, TN), lambda i, j: (layer, 0, j))],
        out_specs=pl.BlockSpec((TM, TN), lambda i, j: (i, j)),
        out_shape=jax.ShapeDtypeStruct((m, n), BF16),
        scratch_shapes=[pltpu.VMEM((TM, k), BF16)],
        compiler_params=_params(2),
        name="ffn_gate",
    )(x, g.reshape(1, k), wg, wu)


def _mm_res_body(y_ref, w_ref, r_ref, o_ref):
    o_ref[...] = r_ref[...] + jnp.dot(y_ref[...], w_ref[...], preferred_element_type=F32)


def _matmul_residual(y, w, layer, r):
    m, k = y.shape
    n = w.shape[2]
    return pl.pallas_call(
        _mm_res_body,
        grid=(m // TM, n // TN),
        in_specs=[pl.BlockSpec((TM, k), lambda i, j: (i, 0)),
                  pl.BlockSpec((None, k, TN), lambda i, j: (layer, 0, j)),
                  pl.BlockSpec((TM, TN), lambda i, j: (i, j))],
        out_specs=pl.BlockSpec((TM, TN), lambda i, j: (i, j)),
        out_shape=jax.ShapeDtypeStruct((m, n), F32),
        compiler_params=_params(2),
        name="matmul_residual",
    )(y, w, r)


def _rmsnorm_body(x_ref, g_ref, o_ref):
    x = x_ref[...]
    ms = jnp.mean(x * x, axis=-1, keepdims=True)
    o_ref[...] = x * lax.rsqrt(ms + EPS) * g_ref[...]


def _rmsnorm(x, g):
    m, k = x.shape
    return pl.pallas_call(
        _rmsnorm_body,
        grid=(m // TM,),
        in_specs=[pl.BlockSpec((TM, k), lambda i: (i, 0)),
                  pl.BlockSpec((1, k), lambda i: (0, 0))],
        out_specs=pl.BlockSpec((TM, k), lambda i: (i, 0)),
        out_shape=jax.ShapeDtypeStruct((m, k), F32),
        compiler_params=_params(1),
        name="final_rmsnorm",
    )(x, g.reshape(1, k))


def _softplus(x):
    return jnp.maximum(x, 0.0) + jnp.log1p(jnp.exp(-jnp.abs(x)))


def _gelu_exact(x):
    return 0.5 * x * (1.0 + lax.erf(x * math.sqrt(0.5)))


def _pair_cols(a, h0, lo):
    return jnp.where(lo, a[:, h0:h0 + 1], a[:, h0 + 1:h0 + 2])


def _ab_body(x_ref, conv0_ref, ssm0_ref, convw_ref, convb_ref, dtb_ref, alog_ref, dskip_ref, gnw_ref,
             lng_ref, lnb_ref, ws_ref, bst_ref, *rest, lr, emit_vn):
    if emit_vn:
        y_ref, ssm_out_ref, vn_ref, xp_ref, h_ref = rest
    else:
        y_ref, ssm_out_ref, xp_ref, h_ref = rest
        vn_ref = None
    c = pl.program_id(1)

    @pl.when(c == 0)
    def _():
        xp_ref[0:8, :] = conv0_ref[0]
        h_ref[...] = ssm0_ref[0]

    rows = x_ref[...]
    if lr < CHUNK:
        rows = jnp.concatenate([rows, jnp.zeros((CHUNK - lr, N_PROJ), F32)], axis=0)
    z = rows[:, COL_Z:COL_Z + D_INNER]
    u = rows[:, COL_U:COL_U + W_B]
    v = rows[:, COL_V:COL_V + W_B]
    dt_raw = rows[:, COL_DT:COL_DT + 128]

    xp_ref[8:8 + CHUNK, :] = rows[:, COL_XBC:COL_XBC + CONV_DIM]
    conv = convb_ref[...]
    for tap in range(CONV_W):
        conv = conv + convw_ref[tap:tap + 1, :] * xp_ref[5 + tap:5 + tap + CHUNK, :]
    xp_ref[0:8, :] = xp_ref[lr:lr + 8, :]
    xbc = conv * jax.nn.sigmoid(conv)
    xs = xbc[:, :D_INNER]
    bm = xbc[:, D_INNER:D_INNER + N_GROUPS_A * D_STATE]
    cm = xbc[:, D_INNER + N_GROUPS_A * D_STATE:]

    row_i = lax.broadcasted_iota(jnp.int32, (CHUNK, CHUNK), 0)
    col_i = lax.broadcasted_iota(jnp.int32, (CHUNK, CHUNK), 1)
    causal = row_i >= col_i
    lo = col_i < HEADDIM_A
    head_lane = lax.broadcasted_iota(jnp.int32, (1, CHUNK), 1) < N_HEADS_A

    a_neg = jnp.where(head_lane, -jnp.exp(alog_ref[...]), 0.0)
    dtv = _softplus(dt_raw + dtb_ref[...])
    if lr < CHUNK:
        dtv = jnp.where(row_i < lr, dtv, 0.0)
    a_cum = jnp.dot(causal.astype(F32), dtv * a_neg, preferred_element_type=F32,
                    precision=lax.Precision.HIGHEST)
    a_cum_t = a_cum.T
    a_last = a_cum[CHUNK - 1:CHUNK, :]
    decay_end = jnp.exp(a_last - a_cum)
    decay_in = jnp.exp(a_cum)
    chunk_decay = jnp.exp(a_last)

    y_parts = []
    for g in range(N_GROUPS_A):
        bg = bm[:, g * D_STATE:(g + 1) * D_STATE].astype(BF16)
        cg = cm[:, g * D_STATE:(g + 1) * D_STATE].astype(BF16)
        cb = lax.dot_general(cg, bg, _NT, preferred_element_type=F32)
        rows_g = slice(g * 8 * HEADDIM_A, (g + 1) * 8 * HEADDIM_A)
        y_off = lax.dot_general(cg, h_ref[rows_g, :].astype(BF16), _NT,
                                preferred_element_type=F32)
        xdtd_parts = []
        for jj in range(4):
            j = g * 4 + jj
            h0 = 2 * j
            xdt = xs[:, j * 128:(j + 1) * 128] * _pair_cols(dtv, h0, lo)
            m_pair = []
            for h in (h0, h0 + 1):
                seg = a_cum[:, h:h + 1] - a_cum_t[h:h + 1, :]
                m_pair.append((cb * jnp.exp(jnp.where(causal, seg, NEG_INF))).astype(BF16))
            lhs = jnp.concatenate(m_pair, axis=1)
            rhs = jnp.concatenate([jnp.where(lo, xdt, 0.0).astype(BF16),
                                   jnp.where(lo, 0.0, xdt).astype(BF16)], axis=0)
            y_diag = jnp.dot(lhs, rhs, preferred_element_type=F32)
            y_parts.append(y_diag + y_off[:, jj * 128:(jj + 1) * 128] * _pair_cols(decay_in, h0, lo))
            xdtd_parts.append((xdt * _pair_cols(decay_end, h0, lo)).astype(BF16))
        xdtd = jnp.concatenate(xdtd_parts, axis=1)
        st = lax.dot_general(xdtd, bg, _TN, preferred_element_type=F32)
        for e in range(8):
            h = g * 8 + e
            r0 = h * HEADDIM_A
            h_ref[r0:r0 + HEADDIM_A, :] = (h_ref[r0:r0 + HEADDIM_A, :] * chunk_decay[0:1, h:h + 1]
                                           + st[e * HEADDIM_A:(e + 1) * HEADDIM_A, :])

    @pl.when(c == pl.num_programs(1) - 1)
    def _():
        ssm_out_ref[0] = h_ref[...]

    y = jnp.concatenate(y_parts, axis=1) + dskip_ref[...] * xs
    gated = y * (z * jax.nn.sigmoid(z))
    ya_parts = []
    gw = D_INNER // N_GROUPS_A
    for g in range(N_GROUPS_A):
        gg = gated[:, g * gw:(g + 1) * gw]
        ms = jnp.mean(gg * gg, axis=-1, keepdims=True)
        ya_parts.append(gg * lax.rsqrt(ms + 1e-5) * gnw_ref[:, g * gw:(g + 1) * gw])

    ug = _gelu_exact(u)
    vg = _gelu_exact(v)
    mu = jnp.mean(vg, axis=-1, keepdims=True)
    xc = vg - mu
    var = jnp.mean(xc * xc, axis=-1, keepdims=True)
    vn = xc * lax.rsqrt(var + 1e-5) * lng_ref[...] + lnb_ref[...]
    vn16 = vn.astype(BF16)
    wb = W_B // N_GROUPS_B
    yb_parts = []
    for g in range(N_GROUPS_B):
        w_causal = jnp.where(causal, ws_ref[g], 0.0).astype(BF16)
        mix = jnp.dot(w_causal, vn16[:, g * wb:(g + 1) * wb], preferred_element_type=F32)
        yb_parts.append(ug[:, g * wb:(g + 1) * wb] * (mix + bst_ref[:, g:g + 1]))

    out = jnp.concatenate(ya_parts + yb_parts, axis=1)
    y_ref[...] = out[:lr].astype(y_ref.dtype)
    if emit_vn:
        vn_ref[...] = vn[:lr]


def _ab_mixer(proj, conv0, ssm0, p, *, n_seq, n_chunk, lr, row_block0, out_dtype, emit_vn):
    n_out_rows = n_seq * n_chunk * lr
    row_map = lambda s, c: (row_block0 + s * n_chunk + c, 0)
    out_map = lambda s, c: (s * n_chunk + c, 0)
    const2 = lambda s, c: (0, 0)
    out_specs = [pl.BlockSpec((lr, 2 * D_INNER), out_map),
                 pl.BlockSpec((1, N_HEADS_A * HEADDIM_A, D_STATE), lambda s, c: (s, 0, 0))]
    out_shape = [jax.ShapeDtypeStruct((n_out_rows, 2 * D_INNER), out_dtype),
                 jax.ShapeDtypeStruct((n_seq, N_HEADS_A * HEADDIM_A, D_STATE), F32)]
    if emit_vn:
        out_specs.append(pl.BlockSpec((lr, W_B), out_map))
        out_shape.append(jax.ShapeDtypeStruct((n_out_rows, W_B), F32))
    return pl.pallas_call(
        functools.partial(_ab_body, lr=lr, emit_vn=emit_vn),
        grid=(n_seq, n_chunk),
        in_specs=[pl.BlockSpec((lr, N_PROJ), row_map),
                  pl.BlockSpec((1, 8, CONV_DIM), lambda s, c: (s, 0, 0)),
                  pl.BlockSpec((1, N_HEADS_A * HEADDIM_A, D_STATE), lambda s, c: (s, 0, 0)),
                  pl.BlockSpec((CONV_W, CONV_DIM), const2),
                  pl.BlockSpec((1, CONV_DIM), const2),
                  pl.BlockSpec((1, 128), const2),
                  pl.BlockSpec((1, 128), const2),
                  pl.BlockSpec((1, D_INNER), const2),
                  pl.BlockSpec((1, D_INNER), const2),
                  pl.BlockSpec((1, W_B), const2),
                  pl.BlockSpec((1, W_B), const2),
                  pl.BlockSpec((N_GROUPS_B, CHUNK, CHUNK), lambda s, c: (0, 0, 0)),
                  pl.BlockSpec((CHUNK, N_GROUPS_B), const2)],
        out_specs=out_specs,
        out_shape=out_shape,
        scratch_shapes=[pltpu.VMEM((8 + CHUNK, CONV_DIM), F32),
                        pltpu.VMEM((N_HEADS_A * HEADDIM_A, D_STATE), F32)],
        compiler_params=_params(2),
        name="ab_mixer",
    )(proj, conv0, ssm0, p["conv_w"], p["conv_b"], p["dt_bias"], p["a_log"], p["d_skip"], p["gnorm_w"],
      p["ln_v_g"], p["ln_v_b"], p["w_spatial"], p["b_spatial_t"])


def _t5_bucket(rel):
    n = jnp.maximum(rel, 0)
    max_exact = NUM_BUCKETS // 2
    nf = jnp.maximum(n, 1).astype(F32)
    large = max_exact + (jnp.log(nf / max_exact) / math.log(MAX_DISTANCE / max_exact)
                         * (NUM_BUCKETS - max_exact)).astype(jnp.int32)
    large = jnp.minimum(large, NUM_BUCKETS - 1)
    return jnp.where(n < max_exact, n, large)


def _bias_body(tab_ref, pt_ref, far_ref, last_ref, own_ref):
    h = pl.program_id(0)

    def lookup(rel):
        bucket = _t5_bucket(rel)
        val = jnp.zeros(rel.shape, F32)
        for b in range(NUM_BUCKETS):
            val = jnp.where(bucket == b, tab_ref[b, h], val)
        return jnp.where(rel >= 0, val * LOG2E, NEG_INF)

    r = lax.broadcasted_iota(jnp.int32, (TQ, TK), 0)
    c = lax.broadcasted_iota(jnp.int32, (TQ, TK), 1)
    pt_ref[0, 0] = jnp.full((TQ, TK), NEG_INF, F32)
    for d in range(2):
        pt_ref[0, 1 + d] = lookup(d * TK + r - c)
    pt_ref[0, 3] = jnp.full((TQ, TK), tab_ref[NUM_BUCKETS - 1, h] * LOG2E, F32)

    t = lax.broadcasted_iota(jnp.int32, (2 * T_S, PAGE_COLS), 0) & (T_S - 1)
    col = lax.broadcasted_iota(jnp.int32, (2 * T_S, PAGE_COLS), 1)
    pos = lax.shift_right_logical(col, 3)
    mine = (col & (N_HEADS_C - 1)) == h
    far_ref[...] = jnp.where(mine, tab_ref[NUM_BUCKETS - 1, h] * LOG2E, NEG_INF)
    last_ref[...] = jnp.where(mine, lookup(PAGE + t - pos), NEG_INF)
    t_own = lax.broadcasted_iota(jnp.int32, (2 * T_S, 128), 0) & (T_S - 1)
    col_own = lax.broadcasted_iota(jnp.int32, (2 * T_S, 128), 1)
    own = lookup(t_own - lax.shift_right_logical(col_own, 3))
    own = jnp.where((col_own & (N_HEADS_C - 1)) == h, own, NEG_INF)
    own_ref[...] = jnp.where(col_own < T_S * N_HEADS_C, own, NEG_INF)


def _bias_tiles(rel_bias):
    rows = lambda n: pl.BlockSpec((2 * T_S, n), lambda h: (h, 0))
    return pl.pallas_call(
        _bias_body,
        grid=(N_HEADS_C,),
        in_specs=[pl.BlockSpec(memory_space=pltpu.SMEM)],
        out_specs=[pl.BlockSpec((1, 4, TQ, TK), lambda h: (h, 0, 0, 0)),
                   rows(PAGE_COLS), rows(PAGE_COLS), rows(128)],
        out_shape=[jax.ShapeDtypeStruct((N_HEADS_C, 4, TQ, TK), F32),
                   jax.ShapeDtypeStruct((128, PAGE_COLS), F32),
                   jax.ShapeDtypeStruct((128, PAGE_COLS), F32),
                   jax.ShapeDtypeStruct((128, 128), F32)],
        compiler_params=_params(1),
        name="t5_bias_tiles",
    )(rel_bias)


def _lambda(lq1_ref, lk1_ref, lq2_ref, lk2_ref, lam_init):
    s1 = jnp.sum(lq1_ref[...] * lk1_ref[...], axis=-1, keepdims=True)
    s2 = jnp.sum(lq2_ref[...] * lk2_ref[...], axis=-1, keepdims=True)
    return jnp.exp(s1) - jnp.exp(s2) + lam_init


def _subln(o, w, lam_init):
    ms = jnp.mean(o * o, axis=-1, keepdims=True)
    return o * lax.rsqrt(ms + 1e-5) * w * (1.0 - lam_init)


def _softmax_step(s2, v16, m_ref, l_ref, acc_ref, idx):
    m_new = jnp.maximum(m_ref[idx], jnp.max(s2, axis=-1, keepdims=True))
    alpha = jnp.exp2(m_ref[idx] - m_new)
    p = jnp.exp2(s2 - _rep(m_new, s2.shape[1] // 128))
    l_ref[idx] = alpha * l_ref[idx] + jnp.sum(p, axis=-1, keepdims=True)
    acc_ref[idx] = (acc_ref[idx] * _rep(alpha, acc_ref.shape[-1] // 128)
                    + jnp.dot(p.astype(BF16), v16, preferred_element_type=F32))
    m_ref[idx] = m_new


def _pattn_body(q_ref, k_ref, v_ref, bias_ref, lq1_ref, lk1_ref, lq2_ref, lk2_ref, subw_ref,
                o_ref, s_ref, m_ref, l_ref, acc_ref, *, lam_init):
    qi = pl.program_id(1)
    q = q_ref[...]
    qs = (q[:, :HEAD_DIM_C], q[:, HEAD_DIM_C:])
    m_ref[...] = jnp.full(m_ref.shape, NEG_INF, F32)
    l_ref[...] = jnp.zeros(l_ref.shape, F32)
    acc_ref[...] = jnp.zeros(acc_ref.shape, F32)
    n_chunks = lax.shift_right_logical(qi, 1) + 1
    chunk = 2 * TK
    last_chunk = N_PROMPT // chunk - 1

    def rows_of(c):
        return pl.ds(pl.multiple_of(jnp.minimum(c, last_chunk) * chunk, chunk), chunk)

    def put_scores(c, slot):
        k = k_ref[rows_of(c), :]
        for mp in range(2):
            s_ref[slot, mp] = lax.dot_general(qs[mp], k[:, mp * HEAD_DIM_C:(mp + 1) * HEAD_DIM_C], _NT,
                                              preferred_element_type=F32)

    def half(c, slot):
        put_scores(c + 1, 1 - slot)
        v = v_ref[rows_of(c), :]
        kind_l = jnp.clip(qi - 2 * c + 1, 0, 3)
        kind_r = jnp.clip(qi - 2 * c, 0, 3)
        bias2 = jnp.concatenate([bias_ref[0, kind_l], bias_ref[0, kind_r]], axis=1)
        for mp in range(2):
            _softmax_step(s_ref[slot, mp] * SCORE_SCALE2 + bias2, v, m_ref, l_ref, acc_ref, mp)

    put_scores(0, 0)

    def body(i, carry):
        half(2 * i, 0)
        half(2 * i + 1, 1)
        return carry

    lax.fori_loop(0, lax.shift_right_logical(n_chunks + 1, 1), body, 0)

    lam = _lambda(lq1_ref, lk1_ref, lq2_ref, lk2_ref, lam_init)
    n_rep = 2 * HEAD_DIM_C // 128
    o = (acc_ref[0] * _rep(1.0 / l_ref[0], n_rep)
         - lam * (acc_ref[1] * _rep(1.0 / l_ref[1], n_rep)))
    o_ref[...] = _subln(o, subw_ref[...], lam_init).astype(o_ref.dtype)


def _prompt_attention(qkv16, bias_pt, p, lam_init):
    dh2 = 2 * HEAD_DIM_C
    vec = lambda n: pl.BlockSpec((1, n), lambda h, i: (0, 0))
    return pl.pallas_call(
        functools.partial(_pattn_body, lam_init=lam_init),
        grid=(N_HEADS_C, N_PROMPT // TQ),
        in_specs=[pl.BlockSpec((TQ, dh2), lambda h, i: (i, h)),
                  pl.BlockSpec((N_PROMPT, dh2), lambda h, i: (0, N_HEADS_C + h)),
                  pl.BlockSpec((N_PROMPT, dh2), lambda h, i: (0, 2 * N_HEADS_C + h)),
                  pl.BlockSpec((1, 4, TQ, TK), lambda h, i: (h, 0, 0, 0)),
                  vec(HEAD_DIM_C), vec(HEAD_DIM_C), vec(HEAD_DIM_C), vec(HEAD_DIM_C), vec(dh2)],
        out_specs=pl.BlockSpec((TQ, dh2), lambda h, i: (i, h)),
        out_shape=jax.ShapeDtypeStruct((N_PROMPT, ATT_W), BF16),
        scratch_shapes=[pltpu.VMEM((2, 2, TQ, 2 * TK), F32),
                        pltpu.VMEM((2, TQ, 128), F32),
                        pltpu.VMEM((2, TQ, 128), F32),
                        pltpu.VMEM((2, TQ, dh2), F32)],
        compiler_params=_params(2),
        name="prompt_attention",
    )(qkv16, qkv16, qkv16, bias_pt, p["lq1"], p["lk1"], p["lq2"], p["lk2"], p["subln_w"])


def _sattn_body(pt_ref, q_ref, ko_ref, vo_ref, *rest, lam_init):
    k_refs = rest[:PAGES_PER_STEP]
    v_refs = rest[PAGES_PER_STEP:2 * PAGES_PER_STEP]
    (far_ref, last_ref, own_ref, lq1_ref, lk1_ref, lq2_ref, lk2_ref, subw_ref,
     o_ref, qs_ref, m_ref, l_ref, acc_ref) = rest[2 * PAGES_PER_STEP:]
    del pt_ref
    j = pl.program_id(1)
    dh2 = 2 * HEAD_DIM_C

    @pl.when(j == 0)
    def _():
        q = q_ref[...]
        first_map = lax.broadcasted_iota(jnp.int32, (T_S, dh2), 1) < HEAD_DIM_C
        blocks = []
        for h in range(N_HEADS_C):
            q_h = q[:, h * dh2:(h + 1) * dh2]
            blocks += [jnp.where(first_map, q_h, 0.0), jnp.where(first_map, 0.0, q_h)]
        qs_ref[...] = jnp.concatenate(blocks, axis=0).astype(BF16)
        m_ref[...] = jnp.full(m_ref.shape, NEG_INF, F32)
        l_ref[...] = jnp.zeros(l_ref.shape, F32)
        acc_ref[...] = jnp.zeros(acc_ref.shape, F32)

    def step(k16, v16, bias2):
        s2 = lax.dot_general(qs_ref[...], k16, _NT, preferred_element_type=F32) * SCORE_SCALE2 + bias2
        _softmax_step(s2, v16, m_ref, l_ref, acc_ref, 0)

    is_last = j == pl.num_programs(1) - 1
    for i in range(PAGES_PER_STEP):
        bias2 = far_ref[...]
        if i == PAGES_PER_STEP - 1:
            bias2 = jnp.where(is_last, last_ref[...], bias2)
        step(k_refs[i][0].astype(BF16), v_refs[i][0].astype(BF16), bias2)

    @pl.when(is_last)
    def _():
        pad = jnp.zeros((128 - T_S * N_HEADS_C, dh2), F32)
        step(jnp.concatenate([ko_ref[0], pad], axis=0).astype(BF16),
             jnp.concatenate([vo_ref[0], pad], axis=0).astype(BF16), own_ref[...])
        lam = _lambda(lq1_ref, lk1_ref, lq2_ref, lk2_ref, lam_init)
        o_all = acc_ref[0] * _rep(1.0 / l_ref[0], dh2 // 128)
        for h in range(N_HEADS_C):
            r0 = h * 2 * T_S
            o_h = o_all[r0:r0 + T_S] - lam * o_all[r0 + T_S:r0 + 2 * T_S]
            o_ref[0, :, h * dh2:(h + 1) * dh2] = _subln(o_h, subw_ref[...], lam_init)


def _sample_attention(page_table, q, k_own, v_own, cache_k, cache_v, far, last, own, p, lam_init):
    dh2 = 2 * HEAD_DIM_C
    own_spec = pl.BlockSpec((1, T_S * N_HEADS_C, dh2), lambda b, j, pt: (b, 0, 0))
    page_spec = lambda i: pl.BlockSpec((1, PAGE_COLS, dh2),
                                       lambda b, j, pt: (pt[b, j * PAGES_PER_STEP + i], 0, 0))
    const = lambda shape: pl.BlockSpec(shape, lambda b, j, pt: (0, 0))
    grid_spec = pltpu.PrefetchScalarGridSpec(
        num_scalar_prefetch=1,
        grid=(N_SEQ_S, N_PAGE_STEPS),
        in_specs=([pl.BlockSpec((T_S, ATT_W), lambda b, j, pt: (b, 0)), own_spec, own_spec]
                  + [page_spec(i) for i in range(PAGES_PER_STEP)]
                  + [page_spec(i) for i in range(PAGES_PER_STEP)]
                  + [const((128, PAGE_COLS)), const((128, PAGE_COLS)), const((128, 128)),
                     const((1, HEAD_DIM_C)), const((1, HEAD_DIM_C)), const((1, HEAD_DIM_C)),
                     const((1, HEAD_DIM_C)), const((1, dh2))]),
        out_specs=pl.BlockSpec((1, T_S, ATT_W), lambda b, j, pt: (b, 0, 0)),
        scratch_shapes=[pltpu.VMEM((128, dh2), BF16),
                        pltpu.VMEM((1, 128, 128), F32),
                        pltpu.VMEM((1, 128, 128), F32),
                        pltpu.VMEM((1, 128, dh2), F32)],
    )
    return pl.pallas_call(
        functools.partial(_sattn_body, lam_init=lam_init),
        grid_spec=grid_spec,
        out_shape=jax.ShapeDtypeStruct((N_SEQ_S, T_S, ATT_W), F32),
        compiler_params=_params(2),
        name="sample_attention",
    )(page_table, q, k_own, v_own, *([cache_k] * PAGES_PER_STEP), *([cache_v] * PAGES_PER_STEP),
      far, last, own, p["lq1"], p["lk1"], p["lq2"], p["lk2"], p["subln_w"])


def kernel(x_prompt, x_sample, cache_k, cache_v, page_table, state_conv, state_ssm, norm_mix, norm_ffn, norm_final, w_in_ab, conv_w, conv_b, dt_bias, a_log, d_skip, gnorm_w, ln_v_g, ln_v_b, w_spatial, b_spatial, w_out_ab, w_qkv, lambda_q1, lambda_k1, lambda_q2, lambda_k2, subln_w, w_o, rel_bias, w_gate, w_up, w_down):
    x = jnp.concatenate([x_prompt.reshape(N_PROMPT, D_MODEL), x_sample.reshape(N_SAMPLE, D_MODEL)], axis=0)

    w_in16 = w_in_ab.astype(BF16)
    n_zx = D_INNER + CONV_DIM
    w_uv16 = w_in16[:, :, n_zx + N_HEADS_A:]
    w_dt16 = jnp.pad(w_in16[:, :, n_zx:n_zx + N_HEADS_A], ((0, 0), (0, 0), (0, TN - N_HEADS_A)))
    w_gate16, w_up16, w_down16 = w_gate.astype(BF16), w_up.astype(BF16), w_down.astype(BF16)
    (proj,) = _norm_matmul(x, norm_mix[0],
                           [(w_in16, 0, 0, n_zx // TN), (w_uv16, 0, 0, 2 * W_B // TN), (w_dt16, 0, 0, 1)], [F32])

    pad_heads = lambda a: jnp.pad(a.reshape(1, N_HEADS_A), ((0, 0), (0, 128 - N_HEADS_A)))
    ab_params = dict(
        conv_w=conv_w[0], conv_b=conv_b[0].reshape(1, CONV_DIM),
        dt_bias=pad_heads(dt_bias[0]), a_log=pad_heads(a_log[0]),
        d_skip=jnp.repeat(d_skip[0], HEADDIM_A).reshape(1, D_INNER),
        gnorm_w=gnorm_w[0].reshape(1, D_INNER),
        ln_v_g=ln_v_g[0].reshape(1, W_B), ln_v_b=ln_v_b[0].reshape(1, W_B),
        w_spatial=w_spatial[0], b_spatial_t=b_spatial[0].T)
    n_state_rows = N_HEADS_A * HEADDIM_A
    y_p, ssm_p = _ab_mixer(
        proj, jnp.zeros((1, 8, CONV_DIM), F32), jnp.zeros((1, n_state_rows, D_STATE), F32), ab_params,
        n_seq=1, n_chunk=N_PROMPT // CHUNK, lr=CHUNK, row_block0=0, out_dtype=BF16, emit_vn=False)
    conv0_s = jnp.pad(state_conv[0], ((0, 0), (8 - (CONV_W - 1), 0), (0, 0)))
    y_s, ssm_s, vn_s = _ab_mixer(
        proj, conv0_s, state_ssm[0].reshape(N_SEQ_S, n_state_rows, D_STATE), ab_params,
        n_seq=N_SEQ_S, n_chunk=1, lr=T_S, row_block0=N_PROMPT // T_S, out_dtype=F32, emit_vn=True)
    y_ab = jnp.concatenate([y_p, y_s.astype(BF16)], axis=0)
    h = _matmul_residual(y_ab, w_out_ab.astype(BF16), 0, x)
    h = _matmul_residual(_ffn_gate(h, norm_ffn[0], w_gate16, w_up16, 0), w_down16, 0, h)

    lam_init = 0.8 - 0.6 * math.exp(-0.3 * 1)
    qkv, qkv16 = _norm_matmul(h, norm_mix[1], [(w_qkv.astype(BF16), 0, 0, 3 * ATT_W // TN)], [F32, BF16])
    att_params = dict(lq1=lambda_q1[0].reshape(1, HEAD_DIM_C), lk1=lambda_k1[0].reshape(1, HEAD_DIM_C),
                      lq2=lambda_q2[0].reshape(1, HEAD_DIM_C), lk2=lambda_k2[0].reshape(1, HEAD_DIM_C),
                      subln_w=subln_w[0].reshape(1, 2 * HEAD_DIM_C))
    bias_pt, bias_far, bias_last, bias_own = _bias_tiles(rel_bias)
    o_p = _prompt_attention(qkv16, bias_pt, att_params, lam_init)
    n_pool = cache_k.shape[1]
    page_shape = (n_pool, PAGE_COLS, 2 * HEAD_DIM_C)
    own_shape = (N_SEQ_S, T_S * N_HEADS_C, 2 * HEAD_DIM_C)
    qkv_s = qkv[N_PROMPT:]
    o_s = _sample_attention(page_table, qkv_s[:, :ATT_W], qkv_s[:, ATT_W:2 * ATT_W].reshape(own_shape),
                            qkv_s[:, 2 * ATT_W:].reshape(own_shape),
                            cache_k[0].reshape(page_shape), cache_v[0].reshape(page_shape),
                            bias_far, bias_last, bias_own, att_params, lam_init)
    o = jnp.concatenate([o_p, o_s.reshape(N_SAMPLE, ATT_W).astype(BF16)], axis=0)
    h = _matmul_residual(o, w_o.astype(BF16), 0, h)
    h = _matmul_residual(_ffn_gate(h, norm_ffn[1], w_gate16, w_up16, 1), w_down16, 1, h)
    y = _rmsnorm(h, norm_final)

    proj_s = proj[N_PROMPT:].reshape(N_SEQ_S, T_S, N_PROJ)
    xbc_cols = slice(COL_XBC, COL_XBC + CONV_DIM)
    kv_shape_p = (1, 1, N_PROMPT, N_HEADS_C, 2 * HEAD_DIM_C)
    kv_shape_s = (1, N_SEQ_S, T_S, N_HEADS_C, 2 * HEAD_DIM_C)
    return (
        y[:N_PROMPT].reshape(1, N_PROMPT, D_MODEL),
        y[N_PROMPT:].reshape(N_SEQ_S, T_S, D_MODEL),
        proj[N_PROMPT - (CONV_W - 1):N_PROMPT, xbc_cols].reshape(1, 1, CONV_W - 1, CONV_DIM),
        ssm_p.reshape(1, 1, N_HEADS_A, HEADDIM_A, D_STATE),
        qkv[:N_PROMPT, ATT_W:2 * ATT_W].reshape(kv_shape_p),
        qkv[:N_PROMPT, 2 * ATT_W:].reshape(kv_shape_p),
        proj_s[:, T_S - (CONV_W - 1):, xbc_cols].reshape(1, N_SEQ_S, CONV_W - 1, CONV_DIM),
        ssm_s.reshape(1, N_SEQ_S, N_HEADS_A, HEADDIM_A, D_STATE),
        vn_s.reshape(1, N_SEQ_S, T_S, W_B),
        qkv[N_PROMPT:, ATT_W:2 * ATT_W].reshape(kv_shape_s),
        qkv[N_PROMPT:, 2 * ATT_W:].reshape(kv_shape_s),
    )
```

```python
import functools
import math

import jax
import jax.numpy as jnp
from jax import lax
from jax.experimental import pallas as pl
from jax.experimental.pallas import tpu as pltpu

F32 = jnp.float32
BF16 = jnp.bfloat16

D_MODEL = 2048
N_PROMPT = 8192
N_SEQ_S = 32
T_S = 8
N_SAMPLE = N_SEQ_S * T_S
N_HEADS_A = 32
HEADDIM_A = 64
N_GROUPS_A = 4
D_STATE = 128
D_INNER = 2048
CONV_DIM = 3072
CONV_W = 4
CHUNK = 128
N_GROUPS_B = 8
W_B = 2048
N_HEADS_C = 8
HEAD_DIM_C = 128
ATT_W = 2048
PAGE = 128
NUM_BUCKETS = 32
MAX_DISTANCE = 128
D_FF = 5632
EPS = 1e-6
ATT_SCALE = HEAD_DIM_C ** -0.5

COL_Z = 0
COL_XBC = 2048
COL_U = 5120
COL_V = 7168
COL_DT = 9216
N_PROJ = 9728

VMEM_LIMIT_BYTES = 60000 * 1024
TM_PROMPT = 1024
TM_SAMPLE = 256
TN_IN = 512
TN_QKV = 1024
TN_FFN = 1408
TN_DOWN = 512
TN_OUT = 1024
TQ = 256
TK = 256
PAGES_PER_STEP = 8
N_PAGE_STEPS = (N_PROMPT // PAGE) // PAGES_PER_STEP
PAGE_COLS = PAGE * N_HEADS_C
LOG2E = math.log2(math.e)
SCORE_SCALE2 = ATT_SCALE * LOG2E
NEG_INF = float("-inf")

_NT = (((1,), (1,)), ((), ()))
_TN = (((0,), (0,)), ((), ()))


def _params(n_grid):
    return pltpu.CompilerParams(dimension_semantics=("arbitrary",) * n_grid,
                                vmem_limit_bytes=VMEM_LIMIT_BYTES)


def _rep(a, n):
    return jnp.concatenate([a] * n, axis=1)


def _rms_to_scratch(x_ref, g_ref, xn_ref, eps):
    x = x_ref[...]
    ms = jnp.mean(x * x, axis=-1, keepdims=True)
    xn_ref[...] = (x * lax.rsqrt(ms + eps) * g_ref[...]).astype(BF16)


def _norm_mm_body(x_ref, g_ref, *rest, starts, out_ranges, lead_scale):
    n_parts = len(starts) - 1
    w_refs, o_refs, xn_ref = rest[:n_parts], rest[n_parts:-1], rest[-1]
    j = pl.program_id(1)

    @pl.when(j == 0)
    def _():
        _rms_to_scratch(x_ref, g_ref, xn_ref, EPS)

    def store(o_ref, r):
        o_ref[...] = r.astype(o_ref.dtype)

    def emit(lo, hi, w_ref):
        r = jnp.dot(xn_ref[...], w_ref[...], preferred_element_type=F32)
        if lead_scale is not None:
            r = r * jnp.where(j < lead_scale[0], lead_scale[1], 1.0)
        for o_ref, (o_lo, o_hi) in zip(o_refs, out_ranges):
            if o_lo <= lo and hi <= o_hi:
                store(o_ref, r)
            elif o_lo < hi and lo < o_hi:
                pl.when((j >= o_lo) & (j < o_hi))(functools.partial(store, o_ref, r))

    if n_parts == 1:
        emit(starts[0], starts[1], w_refs[0])
    else:
        for p, w_ref in enumerate(w_refs):
            pl.when((j >= starts[p]) & (j < starts[p + 1]))(
                functools.partial(emit, starts[p], starts[p + 1], w_ref))


def _norm_matmul(x, g, parts, outs, tm, tn, lead_scale=None):
    m, k = x.shape
    starts = [0]
    for _, _, _, n_tiles in parts:
        starts.append(starts[-1] + n_tiles)

    def w_spec(p):
        _, layer, tile0, n_tiles = parts[p]
        lo = starts[p]
        return pl.BlockSpec((None, k, tn), lambda i, j: (layer, 0, tile0 + jnp.clip(j - lo, 0, n_tiles - 1)))

    def o_spec(lo, hi):
        return pl.BlockSpec((tm, tn), lambda i, j: (i, jnp.clip(j - lo, 0, hi - lo - 1)))

    return pl.pallas_call(
        functools.partial(_norm_mm_body, starts=tuple(starts), out_ranges=tuple((lo, hi) for _, lo, hi in outs),
                          lead_scale=lead_scale),
        grid=(m // tm, starts[-1]),
        in_specs=[pl.BlockSpec((tm, k), lambda i, j: (i, 0)),
                  pl.BlockSpec((1, k), lambda i, j: (0, 0))] + [w_spec(p) for p in range(len(parts))],
        out_specs=[o_spec(lo, hi) for _, lo, hi in outs],
        out_shape=[jax.ShapeDtypeStruct((m, (hi - lo) * tn), dt) for dt, lo, hi in outs],
        scratch_shapes=[pltpu.VMEM((tm, k), BF16)],
        compiler_params=_params(2),
        name="norm_matmul",
    )(x, g.reshape(1, k), *[w for w, _, _, _ in parts])


def _ffn_gate_body(x_ref, g_ref, wg_ref, wu_ref, o_ref, xn_ref):
    @pl.when(pl.program_id(1) == 0)
    def _():
        _rms_to_scratch(x_ref, g_ref, xn_ref, EPS)

    xn = xn_ref[...]
    a = jnp.dot(xn, wg_ref[...], preferred_element_type=F32)
    b = jnp.dot(xn, wu_ref[...], preferred_element_type=F32)
    o_ref[...] = (a * jax.nn.sigmoid(a) * b).astype(o_ref.dtype)


def _ffn_gate(x, g, wg, wu, layer, tm, tn):
    m, k = x.shape
    n = wg.shape[2]
    return pl.pallas_call(
        _ffn_gate_body,
        grid=(m // tm, n // tn),
        in_specs=[pl.BlockSpec((tm, k), lambda i, j: (i, 0)),
                  pl.BlockSpec((1, k), lambda i, j: (0, 0)),
                  pl.BlockSpec((None, k, tn), lambda i, j: (layer, 0, j)),
                  pl.BlockSpec((None, k, tn), lambda i, j: (layer, 0, j))],
        out_specs=pl.BlockSpec((tm, tn), lambda i, j: (i, j)),
        out_shape=jax.ShapeDtypeStruct((m, n), BF16),
        scratch_shapes=[pltpu.VMEM((tm, k), BF16)],
        compiler_params=_params(2),
        name="ffn_gate",
    )(x, g.reshape(1, k), wg, wu)


def _mm_res_body(y_ref, w_ref, r_ref, o_ref):
    o_ref[...] = r_ref[...] + jnp.dot(y_ref[...], w_ref[...], preferred_element_type=F32)


def _matmul_residual(y, w, layer, r, tm, tn):
    m, k = y.shape
    n = w.shape[2]
    return pl.pallas_call(
        _mm_res_body,
        grid=(m // tm, n // tn),
        in_specs=[pl.BlockSpec((tm, k), lambda i, j: (i, 0)),
                  pl.BlockSpec((None, k, tn), lambda i, j: (layer, 0, j)),
                  pl.BlockSpec((tm, tn), lambda i, j: (i, j))],
        out_specs=pl.BlockSpec((tm, tn), lambda i, j: (i, j)),
        out_shape=jax.ShapeDtypeStruct((m, n), F32),
        compiler_params=_params(2),
        name="matmul_residual",
    )(y, w, r)


def _rmsnorm_body(x_ref, g_ref, o_ref):
    x = x_ref[...]
    ms = jnp.mean(x * x, axis=-1, keepdims=True)
    o_ref[...] = x * lax.rsqrt(ms + EPS) * g_ref[...]


def _rmsnorm(x, g, tm):
    m, k = x.shape
    return pl.pallas_call(
        _rmsnorm_body,
        grid=(m // tm,),
        in_specs=[pl.BlockSpec((tm, k), lambda i: (i, 0)),
                  pl.BlockSpec((1, k), lambda i: (0, 0))],
        out_specs=pl.BlockSpec((tm, k), lambda i: (i, 0)),
        out_shape=jax.ShapeDtypeStruct((m, k), F32),
        compiler_params=_params(1),
        name="final_rmsnorm",
    )(x, g.reshape(1, k))


def _softplus(x):
    return jnp.maximum(x, 0.0) + jnp.log1p(jnp.exp(-jnp.abs(x)))


def _gelu_exact(x):
    return 0.5 * x * (1.0 + lax.erf(x * math.sqrt(0.5)))


def _pair_cols(a, h0, lo):
    return jnp.where(lo, a[:, h0:h0 + 1], a[:, h0 + 1:h0 + 2])


def _ab_body(x_ref, conv0_ref, ssm0_ref, convw_ref, convb_ref, dtb_ref, alog_ref, dskip_ref, gnw_ref,
             lng_ref, lnb_ref, ws_ref, bst_ref, *rest, lr, emit_vn):
    if emit_vn:
        y_ref, ssm_out_ref, vn_ref, xp_ref, h_ref = rest
    else:
        y_ref, ssm_out_ref, xp_ref, h_ref = rest
        vn_ref = None
    c = pl.program_id(1)

    @pl.when(c == 0)
    def _():
        xp_ref[0:8, :] = conv0_ref[0]
        h_ref[...] = ssm0_ref[0]

    def pad_rows(a):
        if lr == CHUNK:
            return a
        return jnp.concatenate([a, jnp.zeros((CHUNK - lr, a.shape[1]), a.dtype)], axis=0)

    rows = x_ref[...]
    z = rows[:, COL_Z:COL_Z + D_INNER]
    u = rows[:, COL_U:COL_U + W_B]
    v = rows[:, COL_V:COL_V + W_B]
    dt_raw = rows[:, COL_DT:COL_DT + 128]

    xp_ref[8:8 + lr, :] = rows[:, COL_XBC:COL_XBC + CONV_DIM]
    conv = convb_ref[...]
    for tap in range(CONV_W):
        conv = conv + convw_ref[tap:tap + 1, :] * xp_ref[5 + tap:5 + tap + lr, :]
    xp_ref[0:8, :] = xp_ref[lr:lr + 8, :]
    xbc = pad_rows(conv * jax.nn.sigmoid(conv))
    xs = xbc[:, :D_INNER]
    bm = xbc[:, D_INNER:D_INNER + N_GROUPS_A * D_STATE]
    cm = xbc[:, D_INNER + N_GROUPS_A * D_STATE:]

    row_i = lax.broadcasted_iota(jnp.int32, (CHUNK, CHUNK), 0)
    col_i = lax.broadcasted_iota(jnp.int32, (CHUNK, CHUNK), 1)
    causal = row_i >= col_i
    lo = col_i < HEADDIM_A
    head_lane = lax.broadcasted_iota(jnp.int32, (1, CHUNK), 1) < N_HEADS_A

    a_neg = jnp.where(head_lane, -jnp.exp(alog_ref[...]), 0.0)
    dtv = pad_rows(_softplus(dt_raw + dtb_ref[...]))
    a_cum = jnp.dot(causal.astype(F32), dtv * a_neg, preferred_element_type=F32,
                    precision=lax.Precision.HIGHEST)
    a_cum_t = a_cum.T
    a_last = a_cum[CHUNK - 1:CHUNK, :]
    decay_end = jnp.exp(a_last - a_cum)
    decay_in = jnp.exp(a_cum)
    chunk_decay = jnp.exp(a_last)

    y_parts = []
    for g in range(N_GROUPS_A):
        bg = bm[:, g * D_STATE:(g + 1) * D_STATE].astype(BF16)
        cg = cm[:, g * D_STATE:(g + 1) * D_STATE].astype(BF16)
        cb = lax.dot_general(cg, bg, _NT, preferred_element_type=F32)
        rows_g = slice(g * 8 * HEADDIM_A, (g + 1) * 8 * HEADDIM_A)
        y_off = lax.dot_general(cg, h_ref[rows_g, :].astype(BF16), _NT,
                                preferred_element_type=F32)
        xdtd_parts = []
        for jj in range(4):
            j = g * 4 + jj
            h0 = 2 * j
            xdt = xs[:, j * 128:(j + 1) * 128] * _pair_cols(dtv, h0, lo)
            m_pair = []
            for h in (h0, h0 + 1):
                seg = a_cum[:, h:h + 1] - a_cum_t[h:h + 1, :]
                m_pair.append((cb * jnp.exp(jnp.where(causal, seg, NEG_INF))).astype(BF16))
            lhs = jnp.concatenate(m_pair, axis=1)
            rhs = jnp.concatenate([jnp.where(lo, xdt, 0.0).astype(BF16),
                                   jnp.where(lo, 0.0, xdt).astype(BF16)], axis=0)
            y_diag = jnp.dot(lhs, rhs, preferred_element_type=F32)
            y_parts.append(y_diag + y_off[:, jj * 128:(jj + 1) * 128] * _pair_cols(decay_in, h0, lo))
            xdtd_parts.append((xdt * _pair_cols(decay_end, h0, lo)).astype(BF16))
        xdtd = jnp.concatenate(xdtd_parts, axis=1)
        st = lax.dot_general(xdtd, bg, _TN, preferred_element_type=F32)
        for e in range(8):
            h = g * 8 + e
            r0 = h * HEADDIM_A
            h_ref[r0:r0 + HEADDIM_A, :] = (h_ref[r0:r0 + HEADDIM_A, :] * chunk_decay[0:1, h:h + 1]
                                           + st[e * HEADDIM_A:(e + 1) * HEADDIM_A, :])

    @pl.when(c == pl.num_programs(1) - 1)
    def _():
        ssm_out_ref[0] = h_ref[...]

    y = (jnp.concatenate(y_parts, axis=1) + dskip_ref[...] * xs)[:lr]
    gated = y * (z * jax.nn.sigmoid(z))
    ya_parts = []
    gw = D_INNER // N_GROUPS_A
    for g in range(N_GROUPS_A):
        gg = gated[:, g * gw:(g + 1) * gw]
        ms = jnp.mean(gg * gg, axis=-1, keepdims=True)
        ya_parts.append(gg * lax.rsqrt(ms + 1e-5) * gnw_ref[:, g * gw:(g + 1) * gw])

    ug = _gelu_exact(u)
    vg = _gelu_exact(v)
    mu = jnp.mean(vg, axis=-1, keepdims=True)
    xc = vg - mu
    var = jnp.mean(xc * xc, axis=-1, keepdims=True)
    vn = xc * lax.rsqrt(var + 1e-5) * lng_ref[...] + lnb_ref[...]
    vn16 = pad_rows(vn).astype(BF16)
    wb = W_B // N_GROUPS_B
    yb_parts = []
    for g in range(N_GROUPS_B):
        w_causal = jnp.where(causal, ws_ref[g], 0.0).astype(BF16)
        mix = jnp.dot(w_causal, vn16[:, g * wb:(g + 1) * wb], preferred_element_type=F32)
        yb_parts.append(ug[:, g * wb:(g + 1) * wb] * (mix + bst_ref[:, g:g + 1])[:lr])

    y_ref[...] = jnp.concatenate(ya_parts + yb_parts, axis=1).astype(y_ref.dtype)
    if emit_vn:
        vn_ref[...] = vn


def _ab_mixer(proj, conv0, ssm0, p, *, n_seq, n_chunk, lr, out_dtype, emit_vn):
    n_out_rows = n_seq * n_chunk * lr
    row_map = lambda s, c: (s * n_chunk + c, 0)
    out_map = row_map
    const2 = lambda s, c: (0, 0)
    out_specs = [pl.BlockSpec((lr, 2 * D_INNER), out_map),
                 pl.BlockSpec((1, N_HEADS_A * HEADDIM_A, D_STATE), lambda s, c: (s, 0, 0))]
    out_shape = [jax.ShapeDtypeStruct((n_out_rows, 2 * D_INNER), out_dtype),
                 jax.ShapeDtypeStruct((n_seq, N_HEADS_A * HEADDIM_A, D_STATE), F32)]
    if emit_vn:
        out_specs.append(pl.BlockSpec((lr, W_B), out_map))
        out_shape.append(jax.ShapeDtypeStruct((n_out_rows, W_B), F32))
    return pl.pallas_call(
        functools.partial(_ab_body, lr=lr, emit_vn=emit_vn),
        grid=(n_seq, n_chunk),
        in_specs=[pl.BlockSpec((lr, N_PROJ), row_map),
                  pl.BlockSpec((1, 8, CONV_DIM), lambda s, c: (s, 0, 0)),
                  pl.BlockSpec((1, N_HEADS_A * HEADDIM_A, D_STATE), lambda s, c: (s, 0, 0)),
                  pl.BlockSpec((CONV_W, CONV_DIM), const2),
                  pl.BlockSpec((1, CONV_DIM), const2),
                  pl.BlockSpec((1, 128), const2),
                  pl.BlockSpec((1, 128), const2),
                  pl.BlockSpec((1, D_INNER), const2),
                  pl.BlockSpec((1, D_INNER), const2),
                  pl.BlockSpec((1, W_B), const2),
                  pl.BlockSpec((1, W_B), const2),
                  pl.BlockSpec((N_GROUPS_B, CHUNK, CHUNK), lambda s, c: (0, 0, 0)),
                  pl.BlockSpec((CHUNK, N_GROUPS_B), const2)],
        out_specs=out_specs,
        out_shape=out_shape,
        scratch_shapes=[pltpu.VMEM((8 + CHUNK, CONV_DIM), F32),
                        pltpu.VMEM((N_HEADS_A * HEADDIM_A, D_STATE), F32)],
        compiler_params=_params(2),
        name="ab_mixer",
    )(proj, conv0, ssm0, p["conv_w"], p["conv_b"], p["dt_bias"], p["a_log"], p["d_skip"], p["gnorm_w"],
      p["ln_v_g"], p["ln_v_b"], p["w_spatial"], p["b_spatial_t"])


def _t5_bucket(rel):
    n = jnp.maximum(rel, 0)
    max_exact = NUM_BUCKETS // 2
    nf = jnp.maximum(n, 1).astype(F32)
    large = max_exact + (jnp.log(nf / max_exact) / math.log(MAX_DISTANCE / max_exact)
                         * (NUM_BUCKETS - max_exact)).astype(jnp.int32)
    large = jnp.minimum(large, NUM_BUCKETS - 1)
    return jnp.where(n < max_exact, n, large)


def _bias_body(tab_ref, pt_ref, far_ref, last_ref, own_ref):
    h = pl.program_id(0)

    def lookup(rel):
        bucket = _t5_bucket(rel)
        val = jnp.zeros(rel.shape, F32)
        for b in range(NUM_BUCKETS):
            val = jnp.where(bucket == b, tab_ref[b, h], val)
        return jnp.where(rel >= 0, val * LOG2E, NEG_INF)

    r = lax.broadcasted_iota(jnp.int32, (TQ, TK), 0)
    c = lax.broadcasted_iota(jnp.int32, (TQ, TK), 1)
    pt_ref[0, 0] = jnp.full((TQ, TK), NEG_INF, F32)
    for d in range(2):
        pt_ref[0, 1 + d] = lookup(d * TK + r - c)
    pt_ref[0, 3] = jnp.full((TQ, TK), tab_ref[NUM_BUCKETS - 1, h] * LOG2E, F32)

    t = lax.broadcasted_iota(jnp.int32, (2 * T_S, PAGE_COLS), 0) & (T_S - 1)
    col = lax.broadcasted_iota(jnp.int32, (2 * T_S, PAGE_COLS), 1)
    pos = lax.shift_right_logical(col, 3)
    mine = (col & (N_HEADS_C - 1)) == h
    far_ref[...] = jnp.where(mine, tab_ref[NUM_BUCKETS - 1, h] * LOG2E, NEG_INF)
    last_ref[...] = jnp.where(mine, lookup(PAGE + t - pos), NEG_INF)
    t_own = lax.broadcasted_iota(jnp.int32, (2 * T_S, 128), 0) & (T_S - 1)
    col_own = lax.broadcasted_iota(jnp.int32, (2 * T_S, 128), 1)
    own = lookup(t_own - lax.shift_right_logical(col_own, 3))
    own = jnp.where((col_own & (N_HEADS_C - 1)) == h, own, NEG_INF)
    own_ref[...] = jnp.where(col_own < T_S * N_HEADS_C, own, NEG_INF)


def _bias_tiles(rel_bias):
    rows = lambda n: pl.BlockSpec((2 * T_S, n), lambda h: (h, 0))
    return pl.pallas_call(
        _bias_body,
        grid=(N_HEADS_C,),
        in_specs=[pl.BlockSpec(memory_space=pltpu.SMEM)],
        out_specs=[pl.BlockSpec((1, 4, TQ, TK), lambda h: (h, 0, 0, 0)),
                   rows(PAGE_COLS), rows(PAGE_COLS), rows(128)],
        out_shape=[jax.ShapeDtypeStruct((N_HEADS_C, 4, TQ, TK), F32),
                   jax.ShapeDtypeStruct((128, PAGE_COLS), F32),
                   jax.ShapeDtypeStruct((128, PAGE_COLS), F32),
                   jax.ShapeDtypeStruct((128, 128), F32)],
        compiler_params=_params(1),
        name="t5_bias_tiles",
    )(rel_bias)


def _lambda(lq1_ref, lk1_ref, lq2_ref, lk2_ref, lam_init):
    s1 = jnp.sum(lq1_ref[...] * lk1_ref[...], axis=-1, keepdims=True)
    s2 = jnp.sum(lq2_ref[...] * lk2_ref[...], axis=-1, keepdims=True)
    return jnp.exp(s1) - jnp.exp(s2) + lam_init


def _subln(o, w, lam_init):
    ms = jnp.mean(o * o, axis=-1, keepdims=True)
    return o * lax.rsqrt(ms + 1e-5) * w * (1.0 - lam_init)


def _softmax_step(s2, v16, m_ref, l_ref, acc_ref, idx):
    m_new = jnp.maximum(m_ref[idx], jnp.max(s2, axis=-1, keepdims=True))
    alpha = jnp.exp2(m_ref[idx] - m_new)
    p = jnp.exp2(s2 - _rep(m_new, s2.shape[1] // 128))
    l_ref[idx] = alpha * l_ref[idx] + jnp.sum(p, axis=-1, keepdims=True)
    acc_ref[idx] = (acc_ref[idx] * _rep(alpha, acc_ref.shape[-1] // 128)
                    + jnp.dot(p.astype(BF16), v16, preferred_element_type=F32))
    m_ref[idx] = m_new


def _pattn_body(q_ref, k_ref, v_ref, bias_ref, lq1_ref, lk1_ref, lq2_ref, lk2_ref, subw_ref,
                o_ref, s_ref, m_ref, l_ref, acc_ref, *, lam_init):
    qi = pl.program_id(1)
    q = q_ref[...]
    qs = (q[:, :HEAD_DIM_C], q[:, HEAD_DIM_C:])
    m_ref[...] = jnp.full(m_ref.shape, NEG_INF, F32)
    l_ref[...] = jnp.zeros(l_ref.shape, F32)
    acc_ref[...] = jnp.zeros(acc_ref.shape, F32)
    n_chunks = lax.shift_right_logical(qi, 1) + 1
    chunk = 2 * TK
    last_chunk = N_PROMPT // chunk - 1

    def rows_of(c):
        return pl.ds(pl.multiple_of(jnp.minimum(c, last_chunk) * chunk, chunk), chunk)

    def put_scores(c, slot):
        k = k_ref[rows_of(c), :]
        for mp in range(2):
            s_ref[slot, mp] = lax.dot_general(qs[mp], k[:, mp * HEAD_DIM_C:(mp + 1) * HEAD_DIM_C], _NT,
                                              preferred_element_type=F32)

    def half(c, slot):
        put_scores(c + 1, 1 - slot)
        v = v_ref[rows_of(c), :]
        kind_l = jnp.clip(qi - 2 * c + 1, 0, 3)
        kind_r = jnp.clip(qi - 2 * c, 0, 3)
        bias2 = jnp.concatenate([bias_ref[0, kind_l], bias_ref[0, kind_r]], axis=1)
        for mp in range(2):
            _softmax_step(s_ref[slot, mp] + bias2, v, m_ref, l_ref, acc_ref, mp)

    put_scores(0, 0)

    def body(i, carry):
        half(2 * i, 0)
        half(2 * i + 1, 1)
        return carry

    lax.fori_loop(0, lax.shift_right_logical(n_chunks + 1, 1), body, 0)

    lam = _lambda(lq1_ref, lk1_ref, lq2_ref, lk2_ref, lam_init)
    n_rep = 2 * HEAD_DIM_C // 128
    o = (acc_ref[0] * _rep(1.0 / l_ref[0], n_rep)
         - lam * (acc_ref[1] * _rep(1.0 / l_ref[1], n_rep)))
    o_ref[...] = _subln(o, subw_ref[...], lam_init).astype(o_ref.dtype)


def _prompt_attention(qkv16, bias_pt, p, lam_init):
    dh2 = 2 * HEAD_DIM_C
    vec = lambda n: pl.BlockSpec((1, n), lambda h, i: (0, 0))
    return pl.pallas_call(
        functools.partial(_pattn_body, lam_init=lam_init),
        grid=(N_HEADS_C, N_PROMPT // TQ),
        in_specs=[pl.BlockSpec((TQ, dh2), lambda h, i: (i, h)),
                  pl.BlockSpec((N_PROMPT, dh2), lambda h, i: (0, N_HEADS_C + h)),
                  pl.BlockSpec((N_PROMPT, dh2), lambda h, i: (0, 2 * N_HEADS_C + h)),
                  pl.BlockSpec((1, 4, TQ, TK), lambda h, i: (h, 0, 0, 0)),
                  vec(HEAD_DIM_C), vec(HEAD_DIM_C), vec(HEAD_DIM_C), vec(HEAD_DIM_C), vec(dh2)],
        out_specs=pl.BlockSpec((TQ, dh2), lambda h, i: (i, h)),
        out_shape=jax.ShapeDtypeStruct((N_PROMPT, ATT_W), BF16),
        scratch_shapes=[pltpu.VMEM((2, 2, TQ, 2 * TK), F32),
                        pltpu.VMEM((2, TQ, 128), F32),
                        pltpu.VMEM((2, TQ, 128), F32),
                        pltpu.VMEM((2, TQ, dh2), F32)],
        compiler_params=_params(2),
        name="prompt_attention",
    )(qkv16, qkv16, qkv16, bias_pt, p["lq1"], p["lk1"], p["lq2"], p["lk2"], p["subln_w"])


def _sattn_body(pt_ref, q_ref, ko_ref, vo_ref, *rest, lam_init):
    k_refs = rest[:PAGES_PER_STEP]
    v_refs = rest[PAGES_PER_STEP:2 * PAGES_PER_STEP]
    (far_ref, last_ref, own_ref, lq1_ref, lk1_ref, lq2_ref, lk2_ref, subw_ref,
     o_ref, qs_ref, m_ref, l_ref, acc_ref) = rest[2 * PAGES_PER_STEP:]
    del pt_ref
    j = pl.program_id(1)
    dh2 = 2 * HEAD_DIM_C

    @pl.when(j == 0)
    def _():
        q = q_ref[...]
        first_map = lax.broadcasted_iota(jnp.int32, (T_S, dh2), 1) < HEAD_DIM_C
        blocks = []
        for h in range(N_HEADS_C):
            q_h = q[:, h * dh2:(h + 1) * dh2]
            blocks += [jnp.where(first_map, q_h, 0.0), jnp.where(first_map, 0.0, q_h)]
        qs_ref[...] = jnp.concatenate(blocks, axis=0).astype(BF16)
        m_ref[...] = jnp.full(m_ref.shape, NEG_INF, F32)
        l_ref[...] = jnp.zeros(l_ref.shape, F32)
        acc_ref[...] = jnp.zeros(acc_ref.shape, F32)

    def step(k16, v16, bias2):
        s2 = lax.dot_general(qs_ref[...], k16, _NT, preferred_element_type=F32) + bias2
        _softmax_step(s2, v16, m_ref, l_ref, acc_ref, 0)

    is_last = j == pl.num_programs(1) - 1
    for i in range(PAGES_PER_STEP):
        bias2 = far_ref[...]
        if i == PAGES_PER_STEP - 1:
            bias2 = jnp.where(is_last, last_ref[...], bias2)
        step(k_refs[i][0].astype(BF16), v_refs[i][0].astype(BF16), bias2)

    @pl.when(is_last)
    def _():
        pad = jnp.zeros((128 - T_S * N_HEADS_C, dh2), F32)
        step(jnp.concatenate([ko_ref[0], pad], axis=0).astype(BF16),
             jnp.concatenate([vo_ref[0], pad], axis=0).astype(BF16), own_ref[...])
        lam = _lambda(lq1_ref, lk1_ref, lq2_ref, lk2_ref, lam_init)
        o_all = acc_ref[0] * _rep(1.0 / l_ref[0], dh2 // 128)
        for h in range(N_HEADS_C):
            r0 = h * 2 * T_S
            o_h = o_all[r0:r0 + T_S] - lam * o_all[r0 + T_S:r0 + 2 * T_S]
            o_ref[0, :, h * dh2:(h + 1) * dh2] = _subln(o_h, subw_ref[...], lam_init)


def _sample_attention(page_table, q, k_own, v_own, cache_k, cache_v, far, last, own, p, lam_init):
    dh2 = 2 * HEAD_DIM_C
    own_spec = pl.BlockSpec((1, T_S * N_HEADS_C, dh2), lambda b, j, pt: (b, 0, 0))
    page_spec = lambda i: pl.BlockSpec((1, PAGE_COLS, dh2),
                                       lambda b, j, pt: (pt[b, j * PAGES_PER_STEP + i], 0, 0))
    const = lambda shape: pl.BlockSpec(shape, lambda b, j, pt: (0, 0))
    grid_spec = pltpu.PrefetchScalarGridSpec(
        num_scalar_prefetch=1,
        grid=(N_SEQ_S, N_PAGE_STEPS),
        in_specs=([pl.BlockSpec((T_S, ATT_W), lambda b, j, pt: (b, 0)), own_spec, own_spec]
                  + [page_spec(i) for i in range(PAGES_PER_STEP)]
                  + [page_spec(i) for i in range(PAGES_PER_STEP)]
                  + [const((128, PAGE_COLS)), const((128, PAGE_COLS)), const((128, 128)),
                     const((1, HEAD_DIM_C)), const((1, HEAD_DIM_C)), const((1, HEAD_DIM_C)),
                     const((1, HEAD_DIM_C)), const((1, dh2))]),
        out_specs=pl.BlockSpec((1, T_S, ATT_W), lambda b, j, pt: (b, 0, 0)),
        scratch_shapes=[pltpu.VMEM((128, dh2), BF16),
                        pltpu.VMEM((1, 128, 128), F32),
                        pltpu.VMEM((1, 128, 128), F32),
                        pltpu.VMEM((1, 128, dh2), F32)],
    )
    return pl.pallas_call(
        functools.partial(_sattn_body, lam_init=lam_init),
        grid_spec=grid_spec,
        out_shape=jax.ShapeDtypeStruct((N_SEQ_S, T_S, ATT_W), F32),
        compiler_params=_params(2),
        name="sample_attention",
    )(page_table, q, k_own, v_own, *([cache_k] * PAGES_PER_STEP), *([cache_v] * PAGES_PER_STEP),
      far, last, own, p["lq1"], p["lk1"], p["lq2"], p["lk2"], p["subln_w"])


def kernel(x_prompt, x_sample, cache_k, cache_v, page_table, state_conv, state_ssm, norm_mix, norm_ffn, norm_final, w_in_ab, conv_w, conv_b, dt_bias, a_log, d_skip, gnorm_w, ln_v_g, ln_v_b, w_spatial, b_spatial, w_out_ab, w_qkv, lambda_q1, lambda_k1, lambda_q2, lambda_k2, subln_w, w_o, rel_bias, w_gate, w_up, w_down):
    w_in16 = w_in_ab.astype(BF16)
    n_zx = D_INNER + CONV_DIM
    w_uv16 = w_in16[:, :, n_zx + N_HEADS_A:]
    w_dt16 = jnp.pad(w_in16[:, :, n_zx:n_zx + N_HEADS_A], ((0, 0), (0, 0), (0, TN_IN - N_HEADS_A)))
    w_in_parts = [(w_in16, 0, 0, n_zx // TN_IN), (w_uv16, 0, 0, 2 * W_B // TN_IN), (w_dt16, 0, 0, 1)]
    w_out16, w_qkv16, w_o16 = w_out_ab.astype(BF16), w_qkv.astype(BF16), w_o.astype(BF16)
    w_gate16, w_up16, w_down16 = w_gate.astype(BF16), w_up.astype(BF16), w_down.astype(BF16)
    qkv_tiles = ATT_W // TN_QKV
    qkv_outs = [(BF16, 0, 3 * qkv_tiles), (F32, qkv_tiles, 2 * qkv_tiles), (F32, 2 * qkv_tiles, 3 * qkv_tiles)]

    pad_heads = lambda a: jnp.pad(a.reshape(1, N_HEADS_A), ((0, 0), (0, 128 - N_HEADS_A)))
    ab_params = dict(
        conv_w=conv_w[0], conv_b=conv_b[0].reshape(1, CONV_DIM),
        dt_bias=pad_heads(dt_bias[0]), a_log=pad_heads(a_log[0]),
        d_skip=jnp.repeat(d_skip[0], HEADDIM_A).reshape(1, D_INNER),
        gnorm_w=gnorm_w[0].reshape(1, D_INNER),
        ln_v_g=ln_v_g[0].reshape(1, W_B), ln_v_b=ln_v_b[0].reshape(1, W_B),
        w_spatial=w_spatial[0], b_spatial_t=b_spatial[0].T)
    att_params = dict(lq1=lambda_q1[0].reshape(1, HEAD_DIM_C), lk1=lambda_k1[0].reshape(1, HEAD_DIM_C),
                      lq2=lambda_q2[0].reshape(1, HEAD_DIM_C), lk2=lambda_k2[0].reshape(1, HEAD_DIM_C),
                      subln_w=subln_w[0].reshape(1, 2 * HEAD_DIM_C))
    lam_init = 0.8 - 0.6 * math.exp(-0.3 * 1)
    bias_pt, bias_far, bias_last, bias_own = _bias_tiles(rel_bias)
    n_state_rows = N_HEADS_A * HEADDIM_A

    def layer0(x, tm, mixer):
        (proj,) = _norm_matmul(x, norm_mix[0], w_in_parts, [(F32, 0, N_PROJ // TN_IN)], tm, TN_IN)
        y_ab, *state = mixer(proj)
        h = _matmul_residual(y_ab, w_out16, 0, x, tm, TN_OUT)
        h = _matmul_residual(_ffn_gate(h, norm_ffn[0], w_gate16, w_up16, 0, tm, TN_FFN), w_down16, 0, h, tm, TN_DOWN)
        return proj, state, h

    def layer1(h, tm, attend):
        qkv16, k32, v32 = _norm_matmul(h, norm_mix[1], [(w_qkv16, 0, 0, 3 * qkv_tiles)], qkv_outs, tm, TN_QKV,
                                       lead_scale=(qkv_tiles, SCORE_SCALE2))
        h = _matmul_residual(attend(qkv16, k32, v32), w_o16, 0, h, tm, TN_OUT)
        h = _matmul_residual(_ffn_gate(h, norm_ffn[1], w_gate16, w_up16, 1, tm, TN_FFN), w_down16, 1, h, tm, TN_DOWN)
        return k32, v32, _rmsnorm(h, norm_final, tm)

    def mixer_p(proj):
        return _ab_mixer(proj, jnp.zeros((1, 8, CONV_DIM), F32), jnp.zeros((1, n_state_rows, D_STATE), F32),
                         ab_params, n_seq=1, n_chunk=N_PROMPT // CHUNK, lr=CHUNK, out_dtype=BF16, emit_vn=False)

    proj_p, (ssm_p,), h_p = layer0(x_prompt.reshape(N_PROMPT, D_MODEL), TM_PROMPT, mixer_p)
    k_p, v_p, y_p = layer1(h_p, TM_PROMPT, lambda qkv16, k32, v32: _prompt_attention(qkv16, bias_pt, att_params,
                                                                                        lam_init))

    def mixer_s(proj):
        conv0 = jnp.pad(state_conv[0], ((0, 0), (8 - (CONV_W - 1), 0), (0, 0)))
        y, ssm, vn = _ab_mixer(proj, conv0, state_ssm[0].reshape(N_SEQ_S, n_state_rows, D_STATE), ab_params,
                               n_seq=N_SEQ_S, n_chunk=1, lr=T_S, out_dtype=F32, emit_vn=True)
        return y.astype(BF16), ssm, vn

    def attend_s(qkv16, k32, v32):
        page_shape = (cache_k.shape[1], PAGE_COLS, 2 * HEAD_DIM_C)
        own_shape = (N_SEQ_S, T_S * N_HEADS_C, 2 * HEAD_DIM_C)
        o = _sample_attention(page_table, qkv16[:, :ATT_W].astype(F32), k32.reshape(own_shape),
                              v32.reshape(own_shape), cache_k[0].reshape(page_shape),
                              cache_v[0].reshape(page_shape), bias_far, bias_last, bias_own, att_params, lam_init)
        return o.reshape(N_SAMPLE, ATT_W).astype(BF16)

    proj_s, (ssm_s, vn_s), h_s = layer0(x_sample.reshape(N_SAMPLE, D_MODEL), TM_SAMPLE, mixer_s)
    k_s, v_s, y_s = layer1(h_s, TM_SAMPLE, attend_s)

    xbc_cols = slice(COL_XBC, COL_XBC + CONV_DIM)
    kv_shape_p = (1, 1, N_PROMPT, N_HEADS_C, 2 * HEAD_DIM_C)
    kv_shape_s = (1, N_SEQ_S, T_S, N_HEADS_C, 2 * HEAD_DIM_C)
    return (
        y_p.reshape(1, N_PROMPT, D_MODEL),
        y_s.reshape(N_SEQ_S, T_S, D_MODEL),
        proj_p[N_PROMPT - (CONV_W - 1):, xbc_cols].reshape(1, 1, CONV_W - 1, CONV_DIM),
        ssm_p.reshape(1, 1, N_HEADS_A, HEADDIM_A, D_STATE),
        k_p.reshape(kv_shape_p),
        v_p.reshape(kv_shape_p),
        proj_s.reshape(N_SEQ_S, T_S, N_PROJ)[:, T_S - (CONV_W - 1):, xbc_cols].reshape(
            1, N_SEQ_S, CONV_W - 1, CONV_DIM),
        ssm_s.reshape(1, N_SEQ_S, N_HEADS_A, HEADDIM_A, D_STATE),
        vn_s.reshape(1, N_SEQ_S, T_S, W_B),
        k_s.reshape(kv_shape_s),
        v_s.reshape(kv_shape_s),
    )
```

```python
import functools
import math

import jax
import jax.numpy as jnp
from jax import lax
from jax.experimental import pallas as pl
from jax.experimental.pallas import tpu as pltpu

F32 = jnp.float32
BF16 = jnp.bfloat16

D_MODEL = 2048
N_PROMPT = 8192
N_SEQ_S = 32
T_S = 8
N_SAMPLE = N_SEQ_S * T_S
N_HEADS_A = 32
HEADDIM_A = 64
N_GROUPS_A = 4
D_STATE = 128
D_INNER = 2048
CONV_DIM = 3072
CONV_W = 4
CHUNK = 128
N_GROUPS_B = 8
W_B = 2048
N_HEADS_C = 8
HEAD_DIM_C = 128
ATT_W = 2048
PAGE = 128
NUM_BUCKETS = 32
MAX_DISTANCE = 128
D_FF = 5632
EPS = 1e-6
ATT_SCALE = HEAD_DIM_C ** -0.5

COL_Z = 0
COL_XBC = 2048
COL_U = 5120
COL_V = 7168
COL_DT = 9216
N_PROJ = 9728

VMEM_LIMIT_BYTES = 60000 * 1024
TM_PROMPT = 1024
TM_SAMPLE = 256
TN_IN = 512
TN_QKV = 1024
TN_FFN = 1408
TN_DOWN = 512
TN_OUT = 1024
TQ = 256
TK = 256
PAGES_PER_STEP = 8
N_PAGE_STEPS = (N_PROMPT // PAGE) // PAGES_PER_STEP
PAGE_COLS = PAGE * N_HEADS_C
LOG2E = math.log2(math.e)
SCORE_SCALE2 = ATT_SCALE * LOG2E
NEG_INF = float("-inf")

_NT = (((1,), (1,)), ((), ()))
_TN = (((0,), (0,)), ((), ()))


def _params(n_grid):
    return pltpu.CompilerParams(dimension_semantics=("arbitrary",) * n_grid,
                                vmem_limit_bytes=VMEM_LIMIT_BYTES)


def _rep(a, n):
    return jnp.concatenate([a] * n, axis=1)


def _rms_to_scratch(x_ref, g_ref, xn_ref, eps):
    x = x_ref[...]
    ms = jnp.mean(x * x, axis=-1, keepdims=True)
    xn_ref[...] = (x * lax.rsqrt(ms + eps) * g_ref[...]).astype(BF16)


def _norm_mm_body(x_ref, g_ref, *rest, starts, out_ranges, lead_scale):
    n_parts = len(starts) - 1
    w_refs, o_refs, xn_ref = rest[:n_parts], rest[n_parts:-1], rest[-1]
    j = pl.program_id(1)

    @pl.when(j == 0)
    def _():
        _rms_to_scratch(x_ref, g_ref, xn_ref, EPS)

    def store(o_ref, r):
        o_ref[...] = r.astype(o_ref.dtype)

    def emit(lo, hi, w_ref):
        r = jnp.dot(xn_ref[...], w_ref[...], preferred_element_type=F32)
        if lead_scale is not None:
            r = r * jnp.where(j < lead_scale[0], lead_scale[1], 1.0)
        for o_ref, (o_lo, o_hi) in zip(o_refs, out_ranges):
            if o_lo <= lo and hi <= o_hi:
                store(o_ref, r)
            elif o_lo < hi and lo < o_hi:
                pl.when((j >= o_lo) & (j < o_hi))(functools.partial(store, o_ref, r))

    if n_parts == 1:
        emit(starts[0], starts[1], w_refs[0])
    else:
        for p, w_ref in enumerate(w_refs):
            pl.when((j >= starts[p]) & (j < starts[p + 1]))(
                functools.partial(emit, starts[p], starts[p + 1], w_ref))


def _norm_matmul(x, g, parts, outs, tm, tn, lead_scale=None):
    m, k = x.shape
    starts = [0]
    for _, _, _, n_tiles in parts:
        starts.append(starts[-1] + n_tiles)

    def w_spec(p):
        _, layer, tile0, n_tiles = parts[p]
        lo = starts[p]
        return pl.BlockSpec((None, k, tn), lambda i, j: (layer, 0, tile0 + jnp.clip(j - lo, 0, n_tiles - 1)))

    def o_spec(lo, hi):
        return pl.BlockSpec((tm, tn), lambda i, j: (i, jnp.clip(j - lo, 0, hi - lo - 1)))

    return pl.pallas_call(
        functools.partial(_norm_mm_body, starts=tuple(starts), out_ranges=tuple((lo, hi) for _, lo, hi in outs),
                          lead_scale=lead_scale),
        grid=(m // tm, starts[-1]),
        in_specs=[pl.BlockSpec((tm, k), lambda i, j: (i, 0)),
                  pl.BlockSpec((1, k), lambda i, j: (0, 0))] + [w_spec(p) for p in range(len(parts))],
        out_specs=[o_spec(lo, hi) for _, lo, hi in outs],
        out_shape=[jax.ShapeDtypeStruct((m, (hi - lo) * tn), dt) for dt, lo, hi in outs],
        scratch_shapes=[pltpu.VMEM((tm, k), BF16)],
        compiler_params=_params(2),
        name="norm_matmul",
    )(x, g.reshape(1, k), *[w for w, _, _, _ in parts])


def _ffn_gate_body(x_ref, g_ref, wg_ref, wu_ref, o_ref, xn_ref):
    @pl.when(pl.program_id(1) == 0)
    def _():
        _rms_to_scratch(x_ref, g_ref, xn_ref, EPS)

    xn = xn_ref[...]
    a = jnp.dot(xn, wg_ref[...], preferred_element_type=F32)
    b = jnp.dot(xn, wu_ref[...], preferred_element_type=F32)
    o_ref[...] = (a * jax.nn.sigmoid(a) * b).astype(o_ref.dtype)


def _ffn_gate(x, g, wg, wu, layer, tm, tn):
    m, k = x.shape
    n = wg.shape[2]
    return pl.pallas_call(
        _ffn_gate_body,
        grid=(m // tm, n // tn),
        in_specs=[pl.BlockSpec((tm, k), lambda i, j: (i, 0)),
                  pl.BlockSpec((1, k), lambda i, j: (0, 0)),
                  pl.BlockSpec((None, k, tn), lambda i, j: (layer, 0, j)),
                  pl.BlockSpec((None, k, tn), lambda i, j: (layer, 0, j))],
        out_specs=pl.BlockSpec((tm, tn), lambda i, j: (i, j)),
        out_shape=jax.ShapeDtypeStruct((m, n), BF16),
        scratch_shapes=[pltpu.VMEM((tm, k), BF16)],
        compiler_params=_params(2),
        name="ffn_gate",
    )(x, g.reshape(1, k), wg, wu)


def _mm_res_body(y_ref, w_ref, r_ref, o_ref):
    o_ref[...] = r_ref[...] + jnp.dot(y_ref[...], w_ref[...], preferred_element_type=F32)


def _matmul_residual(y, w, layer, r, tm, tn):
    m, k = y.shape
    n = w.shape[2]
    return pl.pallas_call(
        _mm_res_body,
        grid=(m // tm, n // tn),
        in_specs=[pl.BlockSpec((tm, k), lambda i, j: (i, 0)),
                  pl.BlockSpec((None, k, tn), lambda i, j: (layer, 0, j)),
                  pl.BlockSpec((tm, tn), lambda i, j: (i, j))],
        out_specs=pl.BlockSpec((tm, tn), lambda i, j: (i, j)),
        out_shape=jax.ShapeDtypeStruct((m, n), F32),
        compiler_params=_params(2),
        name="matmul_residual",
    )(y, w, r)


def _rmsnorm_body(x_ref, g_ref, o_ref):
    x = x_ref[...]
    ms = jnp.mean(x * x, axis=-1, keepdims=True)
    o_ref[...] = x * lax.rsqrt(ms + EPS) * g_ref[...]


def _rmsnorm(x, g, tm):
    m, k = x.shape
    return pl.pallas_call(
        _rmsnorm_body,
        grid=(m // tm,),
        in_specs=[pl.BlockSpec((tm, k), lambda i: (i, 0)),
                  pl.BlockSpec((1, k), lambda i: (0, 0))],
        out_specs=pl.BlockSpec((tm, k), lambda i: (i, 0)),
        out_shape=jax.ShapeDtypeStruct((m, k), F32),
        compiler_params=_params(1),
        name="final_rmsnorm",
    )(x, g.reshape(1, k))


def _softplus(x):
    return jnp.maximum(x, 0.0) + jnp.log1p(jnp.exp(-jnp.abs(x)))


def _gelu_exact(x):
    return 0.5 * x * (1.0 + lax.erf(x * math.sqrt(0.5)))


def _pair_cols(a, h0, lo):
    return jnp.where(lo, a[:, h0:h0 + 1], a[:, h0 + 1:h0 + 2])


def _ab_body(x_ref, conv0_ref, ssm0_ref, convw_ref, convb_ref, dtb_ref, alog_ref, dskip_ref, gnw_ref,
             lng_ref, lnb_ref, ws_ref, bst_ref, *rest, lr, emit_vn):
    if emit_vn:
        y_ref, ssm_out_ref, vn_ref, xp_ref, h_ref = rest
    else:
        y_ref, ssm_out_ref, xp_ref, h_ref = rest
        vn_ref = None
    c = pl.program_id(1)

    @pl.when(c == 0)
    def _():
        xp_ref[0:8, :] = conv0_ref[0]
        h_ref[...] = ssm0_ref[0]

    def pad_rows(a):
        if lr == CHUNK:
            return a
        return jnp.concatenate([a, jnp.zeros((CHUNK - lr, a.shape[1]), a.dtype)], axis=0)

    rows = x_ref[...]
    z = rows[:, COL_Z:COL_Z + D_INNER]
    u = rows[:, COL_U:COL_U + W_B]
    v = rows[:, COL_V:COL_V + W_B]
    dt_raw = rows[:, COL_DT:COL_DT + 128]

    xp_ref[8:8 + lr, :] = rows[:, COL_XBC:COL_XBC + CONV_DIM]
    conv = convb_ref[...]
    for tap in range(CONV_W):
        conv = conv + convw_ref[tap:tap + 1, :] * xp_ref[5 + tap:5 + tap + lr, :]
    xp_ref[0:8, :] = xp_ref[lr:lr + 8, :]
    xbc = pad_rows(conv * jax.nn.sigmoid(conv))
    xs = xbc[:, :D_INNER]
    bm = xbc[:, D_INNER:D_INNER + N_GROUPS_A * D_STATE]
    cm = xbc[:, D_INNER + N_GROUPS_A * D_STATE:]

    row_i = lax.broadcasted_iota(jnp.int32, (CHUNK, CHUNK), 0)
    col_i = lax.broadcasted_iota(jnp.int32, (CHUNK, CHUNK), 1)
    causal = row_i >= col_i
    lo = col_i < HEADDIM_A
    head_lane = lax.broadcasted_iota(jnp.int32, (1, CHUNK), 1) < N_HEADS_A

    a_neg = jnp.where(head_lane, -jnp.exp(alog_ref[...]), 0.0)
    dtv = pad_rows(_softplus(dt_raw + dtb_ref[...]))
    a_cum = jnp.dot(causal.astype(F32), dtv * a_neg, preferred_element_type=F32,
                    precision=lax.Precision.HIGHEST)
    a_cum_t = a_cum.T
    a_last = a_cum[CHUNK - 1:CHUNK, :]
    decay_end = jnp.exp(a_last - a_cum)
    decay_in = jnp.exp(a_cum)
    chunk_decay = jnp.exp(a_last)

    y_parts = []
    for g in range(N_GROUPS_A):
        bg = bm[:, g * D_STATE:(g + 1) * D_STATE].astype(BF16)
        cg = cm[:, g * D_STATE:(g + 1) * D_STATE].astype(BF16)
        cb = lax.dot_general(cg, bg, _NT, preferred_element_type=F32)
        rows_g = slice(g * 8 * HEADDIM_A, (g + 1) * 8 * HEADDIM_A)
        y_off = lax.dot_general(cg, h_ref[rows_g, :].astype(BF16), _NT,
                                preferred_element_type=F32)
        xdtd_parts = []
        for jj in range(4):
            j = g * 4 + jj
            h0 = 2 * j
            xdt = xs[:, j * 128:(j + 1) * 128] * _pair_cols(dtv, h0, lo)
            m_pair = []
            for h in (h0, h0 + 1):
                seg = a_cum[:, h:h + 1] - a_cum_t[h:h + 1, :]
                m_pair.append((cb * jnp.exp(jnp.where(causal, seg, NEG_INF))).astype(BF16))
            lhs = jnp.concatenate(m_pair, axis=1)
            rhs = jnp.concatenate([jnp.where(lo, xdt, 0.0).astype(BF16),
                                   jnp.where(lo, 0.0, xdt).astype(BF16)], axis=0)
            y_diag = jnp.dot(lhs, rhs, preferred_element_type=F32)
            y_parts.append(y_diag + y_off[:, jj * 128:(jj + 1) * 128] * _pair_cols(decay_in, h0, lo))
            xdtd_parts.append((xdt * _pair_cols(decay_end, h0, lo)).astype(BF16))
        xdtd = jnp.concatenate(xdtd_parts, axis=1)
        st = lax.dot_general(xdtd, bg, _TN, preferred_element_type=F32)
        for e in range(8):
            h = g * 8 + e
            r0 = h * HEADDIM_A
            h_ref[r0:r0 + HEADDIM_A, :] = (h_ref[r0:r0 + HEADDIM_A, :] * chunk_decay[0:1, h:h + 1]
                                           + st[e * HEADDIM_A:(e + 1) * HEADDIM_A, :])

    @pl.when(c == pl.num_programs(1) - 1)
    def _():
        ssm_out_ref[0] = h_ref[...]

    y = (jnp.concatenate(y_parts, axis=1) + dskip_ref[...] * xs)[:lr]
    gated = y * (z * jax.nn.sigmoid(z))
    ya_parts = []
    gw = D_INNER // N_GROUPS_A
    for g in range(N_GROUPS_A):
        gg = gated[:, g * gw:(g + 1) * gw]
        ms = jnp.mean(gg * gg, axis=-1, keepdims=True)
        ya_parts.append(gg * lax.rsqrt(ms + 1e-5) * gnw_ref[:, g * gw:(g + 1) * gw])

    ug = _gelu_exact(u)
    vg = _gelu_exact(v)
    mu = jnp.mean(vg, axis=-1, keepdims=True)
    xc = vg - mu
    var = jnp.mean(xc * xc, axis=-1, keepdims=True)
    vn = xc * lax.rsqrt(var + 1e-5) * lng_ref[...] + lnb_ref[...]
    vn16 = pad_rows(vn).astype(BF16)
    wb = W_B // N_GROUPS_B
    yb_parts = []
    for g in range(N_GROUPS_B):
        w_causal = jnp.where(causal, ws_ref[g], 0.0).astype(BF16)
        mix = jnp.dot(w_causal, vn16[:, g * wb:(g + 1) * wb], preferred_element_type=F32)
        yb_parts.append(ug[:, g * wb:(g + 1) * wb] * (mix + bst_ref[:, g:g + 1])[:lr])

    y_ref[...] = jnp.concatenate(ya_parts + yb_parts, axis=1).astype(y_ref.dtype)
    if emit_vn:
        vn_ref[...] = vn


def _ab_mixer(proj, conv0, ssm0, p, *, n_seq, n_chunk, lr, out_dtype, emit_vn):
    n_out_rows = n_seq * n_chunk * lr
    row_map = lambda s, c: (s * n_chunk + c, 0)
    out_map = row_map
    const2 = lambda s, c: (0, 0)
    out_specs = [pl.BlockSpec((lr, 2 * D_INNER), out_map),
                 pl.BlockSpec((1, N_HEADS_A * HEADDIM_A, D_STATE), lambda s, c: (s, 0, 0))]
    out_shape = [jax.ShapeDtypeStruct((n_out_rows, 2 * D_INNER), out_dtype),
                 jax.ShapeDtypeStruct((n_seq, N_HEADS_A * HEADDIM_A, D_STATE), F32)]
    if emit_vn:
        out_specs.append(pl.BlockSpec((lr, W_B), out_map))
        out_shape.append(jax.ShapeDtypeStruct((n_out_rows, W_B), F32))
    return pl.pallas_call(
        functools.partial(_ab_body, lr=lr, emit_vn=emit_vn),
        grid=(n_seq, n_chunk),
        in_specs=[pl.BlockSpec((lr, N_PROJ), row_map),
                  pl.BlockSpec((1, 8, CONV_DIM), lambda s, c: (s, 0, 0)),
                  pl.BlockSpec((1, N_HEADS_A * HEADDIM_A, D_STATE), lambda s, c: (s, 0, 0)),
                  pl.BlockSpec((CONV_W, CONV_DIM), const2),
                  pl.BlockSpec((1, CONV_DIM), const2),
                  pl.BlockSpec((1, 128), const2),
                  pl.BlockSpec((1, 128), const2),
                  pl.BlockSpec((1, D_INNER), const2),
                  pl.BlockSpec((1, D_INNER), const2),
                  pl.BlockSpec((1, W_B), const2),
                  pl.BlockSpec((1, W_B), const2),
                  pl.BlockSpec((N_GROUPS_B, CHUNK, CHUNK), lambda s, c: (0, 0, 0)),
                  pl.BlockSpec((CHUNK, N_GROUPS_B), const2)],
        out_specs=out_specs,
        out_shape=out_shape,
        scratch_shapes=[pltpu.VMEM((8 + CHUNK, CONV_DIM), F32),
                        pltpu.VMEM((N_HEADS_A * HEADDIM_A, D_STATE), F32)],
        compiler_params=_params(2),
        name="ab_mixer",
    )(proj, conv0, ssm0, p["conv_w"], p["conv_b"], p["dt_bias"], p["a_log"], p["d_skip"], p["gnorm_w"],
      p["ln_v_g"], p["ln_v_b"], p["w_spatial"], p["b_spatial_t"])


def _t5_bucket(rel):
    n = jnp.maximum(rel, 0)
    max_exact = NUM_BUCKETS // 2
    nf = jnp.maximum(n, 1).astype(F32)
    large = max_exact + (jnp.log(nf / max_exact) / math.log(MAX_DISTANCE / max_exact)
                         * (NUM_BUCKETS - max_exact)).astype(jnp.int32)
    large = jnp.minimum(large, NUM_BUCKETS - 1)
    return jnp.where(n < max_exact, n, large)


def _bias_body(tab_ref, pt_ref, far_ref, last_ref, own_ref):
    h = pl.program_id(0)

    def lookup(rel):
        bucket = _t5_bucket(rel)
        val = jnp.zeros(rel.shape, F32)
        for b in range(NUM_BUCKETS):
            val = jnp.where(bucket == b, tab_ref[b, h], val)
        return jnp.where(rel >= 0, val * LOG2E, NEG_INF)

    r = lax.broadcasted_iota(jnp.int32, (TQ, TK), 0)
    c = lax.broadcasted_iota(jnp.int32, (TQ, TK), 1)
    pt_ref[0, 0] = jnp.full((TQ, TK), NEG_INF, F32)
    for d in range(2):
        pt_ref[0, 1 + d] = lookup(d * TK + r - c)
    pt_ref[0, 3] = jnp.full((TQ, TK), tab_ref[NUM_BUCKETS - 1, h] * LOG2E, F32)

    t = lax.broadcasted_iota(jnp.int32, (2 * T_S, PAGE_COLS), 0) & (T_S - 1)
    col = lax.broadcasted_iota(jnp.int32, (2 * T_S, PAGE_COLS), 1)
    pos = lax.shift_right_logical(col, 3)
    mine = (col & (N_HEADS_C - 1)) == h
    far_ref[...] = jnp.where(mine, tab_ref[NUM_BUCKETS - 1, h] * LOG2E, NEG_INF)
    last_ref[...] = jnp.where(mine, lookup(PAGE + t - pos), NEG_INF)
    t_own = lax.broadcasted_iota(jnp.int32, (2 * T_S, 128), 0) & (T_S - 1)
    col_own = lax.broadcasted_iota(jnp.int32, (2 * T_S, 128), 1)
    own = lookup(t_own - lax.shift_right_logical(col_own, 3))
    own = jnp.where((col_own & (N_HEADS_C - 1)) == h, own, NEG_INF)
    own_ref[...] = jnp.where(col_own < T_S * N_HEADS_C, own, NEG_INF)


def _bias_tiles(rel_bias):
    rows = lambda n: pl.BlockSpec((2 * T_S, n), lambda h: (h, 0))
    return pl.pallas_call(
        _bias_body,
        grid=(N_HEADS_C,),
        in_specs=[pl.BlockSpec(memory_space=pltpu.SMEM)],
        out_specs=[pl.BlockSpec((1, 4, TQ, TK), lambda h: (h, 0, 0, 0)),
                   rows(PAGE_COLS), rows(PAGE_COLS), rows(128)],
        out_shape=[jax.ShapeDtypeStruct((N_HEADS_C, 4, TQ, TK), F32),
                   jax.ShapeDtypeStruct((128, PAGE_COLS), F32),
                   jax.ShapeDtypeStruct((128, PAGE_COLS), F32),
                   jax.ShapeDtypeStruct((128, 128), F32)],
        compiler_params=_params(1),
        name="t5_bias_tiles",
    )(rel_bias)


def _lambda(lq1_ref, lk1_ref, lq2_ref, lk2_ref, lam_init):
    s1 = jnp.sum(lq1_ref[...] * lk1_ref[...], axis=-1, keepdims=True)
    s2 = jnp.sum(lq2_ref[...] * lk2_ref[...], axis=-1, keepdims=True)
    return jnp.exp(s1) - jnp.exp(s2) + lam_init


def _subln(o, w, lam_init):
    ms = jnp.mean(o * o, axis=-1, keepdims=True)
    return o * lax.rsqrt(ms + 1e-5) * w * (1.0 - lam_init)


def _softmax_step(s2, v16, m_ref, l_ref, acc_ref, idx):
    m_new = jnp.maximum(m_ref[idx], jnp.max(s2, axis=-1, keepdims=True))
    alpha = jnp.exp2(m_ref[idx] - m_new)
    p = jnp.exp2(s2 - _rep(m_new, s2.shape[1] // 128))
    l_ref[idx] = alpha * l_ref[idx] + jnp.sum(p, axis=-1, keepdims=True)
    acc_ref[idx] = (acc_ref[idx] * _rep(alpha, acc_ref.shape[-1] // 128)
                    + jnp.dot(p.astype(BF16), v16, preferred_element_type=F32))
    m_ref[idx] = m_new


def _pattn_compute(qi, q_ref, k_ref, v_ref, bias_ref, lam_refs, subw_ref, o_ref, s_ref, m_ref, l_ref, acc_ref,
                   lam_init):
    q = q_ref[...]
    qs = (q[:, :HEAD_DIM_C], q[:, HEAD_DIM_C:])
    m_ref[...] = jnp.full(m_ref.shape, NEG_INF, F32)
    l_ref[...] = jnp.zeros(l_ref.shape, F32)
    acc_ref[...] = jnp.zeros(acc_ref.shape, F32)
    n_chunks = lax.shift_right_logical(qi, 1) + 1
    chunk = 2 * TK
    last_chunk = N_PROMPT // chunk - 1

    def rows_of(c):
        return pl.ds(pl.multiple_of(jnp.minimum(c, last_chunk) * chunk, chunk), chunk)

    def put_scores(c, slot):
        k = k_ref[rows_of(c), :]
        for mp in range(2):
            s_ref[slot, mp] = lax.dot_general(qs[mp], k[:, mp * HEAD_DIM_C:(mp + 1) * HEAD_DIM_C], _NT,
                                              preferred_element_type=F32)

    def half(c, slot):
        put_scores(c + 1, 1 - slot)
        v = v_ref[rows_of(c), :]
        kind_l = jnp.clip(qi - 2 * c + 1, 0, 3)
        kind_r = jnp.clip(qi - 2 * c, 0, 3)
        bias2 = jnp.concatenate([bias_ref[0, kind_l], bias_ref[0, kind_r]], axis=1)
        for mp in range(2):
            _softmax_step(s_ref[slot, mp] + bias2, v, m_ref, l_ref, acc_ref, mp)

    put_scores(0, 0)

    def body(i, carry):
        half(2 * i, 0)
        half(2 * i + 1, 1)
        return carry

    lax.fori_loop(0, lax.shift_right_logical(n_chunks + 1, 1), body, 0)

    lam = _lambda(*lam_refs, lam_init)
    n_rep = 2 * HEAD_DIM_C // 128
    o = (acc_ref[0] * _rep(1.0 / l_ref[0], n_rep)
         - lam * (acc_ref[1] * _rep(1.0 / l_ref[1], n_rep)))
    o_ref[...] = _subln(o, subw_ref[...], lam_init).astype(o_ref.dtype)


def _sattn_compute(j, q_ref, ko_ref, vo_ref, k_refs, v_refs, far_ref, last_ref, own_ref, lam_refs, subw_ref,
                   o_ref, qs_ref, m_ref, l_ref, acc_ref, al_ref, sc_ref, lam_init):
    dh2 = 2 * HEAD_DIM_C

    @pl.when(j == 0)
    def _():
        q = q_ref[...]
        first_map = lax.broadcasted_iota(jnp.int32, (T_S, dh2), 1) < HEAD_DIM_C
        blocks = []
        for h in range(N_HEADS_C):
            q_h = q[:, h * dh2:(h + 1) * dh2]
            blocks += [jnp.where(first_map, q_h, 0.0), jnp.where(first_map, 0.0, q_h)]
        qs_ref[...] = jnp.concatenate(blocks, axis=0).astype(BF16)
        m_ref[...] = jnp.full(m_ref.shape, NEG_INF, F32)
        l_ref[...] = jnp.zeros(l_ref.shape, F32)
        acc_ref[...] = jnp.zeros(acc_ref.shape, F32)

    def scores(k16, bias2):
        return lax.dot_general(qs_ref[...], k16, _NT, preferred_element_type=F32) + bias2

    def step(k16, v16, bias2):
        _softmax_step(scores(k16, bias2), v16, m_ref, l_ref, acc_ref, 0)

    is_last = j == N_PAGE_STEPS - 1
    m_cur = None
    for i in range(PAGES_PER_STEP):
        bias2 = far_ref[...]
        if i == PAGES_PER_STEP - 1:
            bias2 = jnp.where(is_last, last_ref[...], bias2)
        s2 = scores(k_refs[i][0].astype(BF16), bias2)
        sc_ref[i] = s2
        m_i = jnp.max(s2, axis=-1, keepdims=True)
        m_cur = m_i if m_cur is None else jnp.maximum(m_cur, m_i)
    m_new = jnp.maximum(m_ref[0], m_cur)
    al_ref[0] = jnp.exp2(m_ref[0] - m_new)
    m_ref[0] = m_new

    @pl.when(j >= 0)
    def _():
        alpha = al_ref[0]
        shift = _rep(m_ref[0], PAGE_COLS // 128)
        l_new = alpha * l_ref[0]
        acc_new = acc_ref[0] * _rep(alpha, dh2 // 128)
        for i in range(PAGES_PER_STEP):
            p = jnp.exp2(sc_ref[i] - shift)
            l_new = l_new + jnp.sum(p, axis=-1, keepdims=True)
            acc_new = acc_new + jnp.dot(p.astype(BF16), v_refs[i][0].astype(BF16), preferred_element_type=F32)
        l_ref[0] = l_new
        acc_ref[0] = acc_new

    @pl.when(is_last)
    def _():
        pad = jnp.zeros((128 - T_S * N_HEADS_C, dh2), F32)
        step(jnp.concatenate([ko_ref[0], pad], axis=0).astype(BF16),
             jnp.concatenate([vo_ref[0], pad], axis=0).astype(BF16), own_ref[...])
        lam = _lambda(*lam_refs, lam_init)
        o_all = acc_ref[0] * _rep(1.0 / l_ref[0], dh2 // 128)
        for h in range(N_HEADS_C):
            r0 = h * 2 * T_S
            o_h = o_all[r0:r0 + T_S] - lam * o_all[r0 + T_S:r0 + 2 * T_S]
            o_ref[0, :, h * dh2:(h + 1) * dh2] = _subln(o_h, subw_ref[...], lam_init)


def _attn_body(pt_ref, q_ref, k_ref, v_ref, bias_ref, qsm_ref, ko_ref, vo_ref, *rest, lam_init):
    k_refs = rest[:PAGES_PER_STEP]
    v_refs = rest[PAGES_PER_STEP:2 * PAGES_PER_STEP]
    (far_ref, last_ref, own_ref, lq1_ref, lk1_ref, lq2_ref, lk2_ref, subw_ref, op_ref, os_ref,
     s_ref, m_ref, l_ref, acc_ref, qs_ref, ms_ref, ls_ref, accs_ref, als_ref, sc_ref) = rest[2 * PAGES_PER_STEP:]
    del pt_ref
    lam_refs = (lq1_ref, lk1_ref, lq2_ref, lk2_ref)
    qi = pl.program_id(1)
    t = pl.program_id(0) * pl.num_programs(1) + qi
    _sattn_compute(t & (N_PAGE_STEPS - 1), qsm_ref, ko_ref, vo_ref, k_refs, v_refs, far_ref, last_ref, own_ref,
                   lam_refs, subw_ref, os_ref, qs_ref, ms_ref, ls_ref, accs_ref, als_ref, sc_ref, lam_init)
    _pattn_compute(qi, q_ref, k_ref, v_ref, bias_ref, lam_refs, subw_ref, op_ref, s_ref, m_ref, l_ref, acc_ref,
                   lam_init)


def _attention(qkv16, bias_pt, page_table, q_s, k_own, v_own, cache_k, cache_v, far, last, own, p, lam_init):
    dh2 = 2 * HEAD_DIM_C
    n_q = N_PROMPT // TQ
    assert N_HEADS_C * n_q == N_SEQ_S * N_PAGE_STEPS and N_PAGE_STEPS & (N_PAGE_STEPS - 1) == 0
    seq = lambda h, i: (h * n_q + i) // N_PAGE_STEPS
    grp = lambda h, i: (h * n_q + i) % N_PAGE_STEPS
    resident = lambda col0: pl.BlockSpec((N_PROMPT, dh2), lambda h, i, pt: (0, col0 + h),
                                         pipeline_mode=pl.Buffered(1))
    own_spec = pl.BlockSpec((1, T_S * N_HEADS_C, dh2), lambda h, i, pt: (seq(h, i), 0, 0))
    page_spec = lambda n: pl.BlockSpec(
        (1, PAGE_COLS, dh2), lambda h, i, pt: (pt[seq(h, i), grp(h, i) * PAGES_PER_STEP + n], 0, 0))
    const = lambda shape: pl.BlockSpec(shape, lambda h, i, pt: (0, 0))
    grid_spec = pltpu.PrefetchScalarGridSpec(
        num_scalar_prefetch=1,
        grid=(N_HEADS_C, n_q),
        in_specs=([pl.BlockSpec((TQ, dh2), lambda h, i, pt: (i, h)), resident(N_HEADS_C), resident(2 * N_HEADS_C),
                   pl.BlockSpec((1, 4, TQ, TK), lambda h, i, pt: (h, 0, 0, 0)),
                   pl.BlockSpec((T_S, ATT_W), lambda h, i, pt: (seq(h, i), 0)), own_spec, own_spec]
                  + [page_spec(n) for n in range(PAGES_PER_STEP)]
                  + [page_spec(n) for n in range(PAGES_PER_STEP)]
                  + [const((128, PAGE_COLS)), const((128, PAGE_COLS)), const((128, 128)),
                     const((1, HEAD_DIM_C)), const((1, HEAD_DIM_C)), const((1, HEAD_DIM_C)),
                     const((1, HEAD_DIM_C)), const((1, dh2))]),
        out_specs=[pl.BlockSpec((TQ, dh2), lambda h, i, pt: (i, h)),
                   pl.BlockSpec((1, T_S, ATT_W), lambda h, i, pt: (seq(h, i), 0, 0))],
        scratch_shapes=[pltpu.VMEM((2, 2, TQ, 2 * TK), F32),
                        pltpu.VMEM((2, TQ, 128), F32),
                        pltpu.VMEM((2, TQ, 128), F32),
                        pltpu.VMEM((2, TQ, dh2), F32),
                        pltpu.VMEM((128, dh2), BF16),
                        pltpu.VMEM((1, 128, 128), F32),
                        pltpu.VMEM((1, 128, 128), F32),
                        pltpu.VMEM((1, 128, dh2), F32),
                        pltpu.VMEM((1, 128, 128), F32),
                        pltpu.VMEM((PAGES_PER_STEP, 128, PAGE_COLS), F32)],
    )
    return pl.pallas_call(
        functools.partial(_attn_body, lam_init=lam_init),
        grid_spec=grid_spec,
        out_shape=[jax.ShapeDtypeStruct((N_PROMPT, ATT_W), BF16),
                   jax.ShapeDtypeStruct((N_SEQ_S, T_S, ATT_W), F32)],
        compiler_params=_params(2),
        name="attention",
    )(page_table, qkv16, qkv16, qkv16, bias_pt, q_s, k_own, v_own,
      *([cache_k] * PAGES_PER_STEP), *([cache_v] * PAGES_PER_STEP),
      far, last, own, p["lq1"], p["lk1"], p["lq2"], p["lk2"], p["subln_w"])


def kernel(x_prompt, x_sample, cache_k, cache_v, page_table, state_conv, state_ssm, norm_mix, norm_ffn, norm_final, w_in_ab, conv_w, conv_b, dt_bias, a_log, d_skip, gnorm_w, ln_v_g, ln_v_b, w_spatial, b_spatial, w_out_ab, w_qkv, lambda_q1, lambda_k1, lambda_q2, lambda_k2, subln_w, w_o, rel_bias, w_gate, w_up, w_down):
    w_in16 = w_in_ab.astype(BF16)
    n_zx = D_INNER + CONV_DIM
    w_uv16 = w_in16[:, :, n_zx + N_HEADS_A:]
    w_dt16 = jnp.pad(w_in16[:, :, n_zx:n_zx + N_HEADS_A], ((0, 0), (0, 0), (0, TN_IN - N_HEADS_A)))
    w_in_parts = [(w_in16, 0, 0, n_zx // TN_IN), (w_uv16, 0, 0, 2 * W_B // TN_IN), (w_dt16, 0, 0, 1)]
    w_out16, w_qkv16, w_o16 = w_out_ab.astype(BF16), w_qkv.astype(BF16), w_o.astype(BF16)
    w_gate16, w_up16, w_down16 = w_gate.astype(BF16), w_up.astype(BF16), w_down.astype(BF16)
    qkv_tiles = ATT_W // TN_QKV
    qkv_outs = [(BF16, 0, 3 * qkv_tiles), (F32, qkv_tiles, 2 * qkv_tiles), (F32, 2 * qkv_tiles, 3 * qkv_tiles)]

    pad_heads = lambda a: jnp.pad(a.reshape(1, N_HEADS_A), ((0, 0), (0, 128 - N_HEADS_A)))
    ab_params = dict(
        conv_w=conv_w[0], conv_b=conv_b[0].reshape(1, CONV_DIM),
        dt_bias=pad_heads(dt_bias[0]), a_log=pad_heads(a_log[0]),
        d_skip=jnp.repeat(d_skip[0], HEADDIM_A).reshape(1, D_INNER),
        gnorm_w=gnorm_w[0].reshape(1, D_INNER),
        ln_v_g=ln_v_g[0].reshape(1, W_B), ln_v_b=ln_v_b[0].reshape(1, W_B),
        w_spatial=w_spatial[0], b_spatial_t=b_spatial[0].T)
    att_params = dict(lq1=lambda_q1[0].reshape(1, HEAD_DIM_C), lk1=lambda_k1[0].reshape(1, HEAD_DIM_C),
                      lq2=lambda_q2[0].reshape(1, HEAD_DIM_C), lk2=lambda_k2[0].reshape(1, HEAD_DIM_C),
                      subln_w=subln_w[0].reshape(1, 2 * HEAD_DIM_C))
    lam_init = 0.8 - 0.6 * math.exp(-0.3 * 1)
    bias_pt, bias_far, bias_last, bias_own = _bias_tiles(rel_bias)
    n_state_rows = N_HEADS_A * HEADDIM_A

    def layer0(x, tm, mixer):
        (proj,) = _norm_matmul(x, norm_mix[0], w_in_parts, [(F32, 0, N_PROJ // TN_IN)], tm, TN_IN)
        y_ab, *state = mixer(proj)
        h = _matmul_residual(y_ab, w_out16, 0, x, tm, TN_OUT)
        h = _matmul_residual(_ffn_gate(h, norm_ffn[0], w_gate16, w_up16, 0, tm, TN_FFN), w_down16, 0, h, tm, TN_DOWN)
        return proj, state, h

    def qkv_stage(h, tm):
        return _norm_matmul(h, norm_mix[1], [(w_qkv16, 0, 0, 3 * qkv_tiles)], qkv_outs, tm, TN_QKV,
                            lead_scale=(qkv_tiles, SCORE_SCALE2))

    def layer1_tail(h, o, tm):
        h = _matmul_residual(o, w_o16, 0, h, tm, TN_OUT)
        h = _matmul_residual(_ffn_gate(h, norm_ffn[1], w_gate16, w_up16, 1, tm, TN_FFN), w_down16, 1, h, tm, TN_DOWN)
        return _rmsnorm(h, norm_final, tm)

    def mixer_p(proj):
        return _ab_mixer(proj, jnp.zeros((1, 8, CONV_DIM), F32), jnp.zeros((1, n_state_rows, D_STATE), F32),
                         ab_params, n_seq=1, n_chunk=N_PROMPT // CHUNK, lr=CHUNK, out_dtype=BF16, emit_vn=False)

    def mixer_s(proj):
        conv0 = jnp.pad(state_conv[0], ((0, 0), (8 - (CONV_W - 1), 0), (0, 0)))
        y, ssm, vn = _ab_mixer(proj, conv0, state_ssm[0].reshape(N_SEQ_S, n_state_rows, D_STATE), ab_params,
                               n_seq=N_SEQ_S, n_chunk=1, lr=T_S, out_dtype=F32, emit_vn=True)
        return y.astype(BF16), ssm, vn

    proj_p, (ssm_p,), h_p = layer0(x_prompt.reshape(N_PROMPT, D_MODEL), TM_PROMPT, mixer_p)
    proj_s, (ssm_s, vn_s), h_s = layer0(x_sample.reshape(N_SAMPLE, D_MODEL), TM_SAMPLE, mixer_s)
    qkv16_p, k_p, v_p = qkv_stage(h_p, TM_PROMPT)
    qkv16_s, k_s, v_s = qkv_stage(h_s, TM_SAMPLE)
    page_shape = (cache_k.shape[1], PAGE_COLS, 2 * HEAD_DIM_C)
    own_shape = (N_SEQ_S, T_S * N_HEADS_C, 2 * HEAD_DIM_C)
    o_p, o_s = _attention(qkv16_p, bias_pt, page_table, qkv16_s[:, :ATT_W].astype(F32), k_s.reshape(own_shape),
                          v_s.reshape(own_shape), cache_k[0].reshape(page_shape), cache_v[0].reshape(page_shape),
                          bias_far, bias_last, bias_own, att_params, lam_init)
    y_p = layer1_tail(h_p, o_p, TM_PROMPT)
    y_s = layer1_tail(h_s, o_s.reshape(N_SAMPLE, ATT_W).astype(BF16), TM_SAMPLE)

    xbc_cols = slice(COL_XBC, COL_XBC + CONV_DIM)
    kv_shape_p = (1, 1, N_PROMPT, N_HEADS_C, 2 * HEAD_DIM_C)
    kv_shape_s = (1, N_SEQ_S, T_S, N_HEADS_C, 2 * HEAD_DIM_C)
    return (
        y_p.reshape(1, N_PROMPT, D_MODEL),
        y_s.reshape(N_SEQ_S, T_S, D_MODEL),
        proj_p[N_PROMPT - (CONV_W - 1):, xbc_cols].reshape(1, 1, CONV_W - 1, CONV_DIM),
        ssm_p.reshape(1, 1, N_HEADS_A, HEADDIM_A, D_STATE),
        k_p.reshape(kv_shape_p),
        v_p.reshape(kv_shape_p),
        proj_s.reshape(N_SEQ_S, T_S, N_PROJ)[:, T_S - (CONV_W - 1):, xbc_cols].reshape(
            1, N_SEQ_S, CONV_W - 1, CONV_DIM),
        ssm_s.reshape(1, N_SEQ_S, N_HEADS_A, HEADDIM_A, D_STATE),
        vn_s.reshape(1, N_SEQ_S, T_S, W_B),
        k_s.reshape(kv_shape_s),
        v_s.reshape(kv_shape_s),
    )
```

```python
import functools
import math

import jax
import jax.numpy as jnp
from jax import lax
from jax.experimental import pallas as pl
from jax.experimental.pallas import tpu as pltpu

F32 = jnp.float32
BF16 = jnp.bfloat16

D_MODEL = 2048
N_PROMPT = 8192
N_SEQ_S = 32
T_S = 8
N_SAMPLE = N_SEQ_S * T_S
N_HEADS_A = 32
HEADDIM_A = 64
N_GROUPS_A = 4
D_STATE = 128
D_INNER = 2048
CONV_DIM = 3072
CONV_W = 4
CHUNK = 128
SCAN_ROWS_MIN = 16
N_GROUPS_B = 8
W_B = 2048
N_HEADS_C = 8
HEAD_DIM_C = 128
ATT_W = 2048
PAGE = 128
NUM_BUCKETS = 32
MAX_DISTANCE = 128
D_FF = 5632
EPS = 1e-6
ATT_SCALE = HEAD_DIM_C ** -0.5

COL_Z = 0
COL_XBC = 2048
COL_U = 5120
COL_V = 7168
COL_DT = 9216
N_PROJ = 9728

VMEM_LIMIT_BYTES = 60000 * 1024
TM_PROMPT = 1024
TM_SAMPLE = 256
TN_IN = 512
TN_QKV = 1024
TN_FFN = 1408
TN_DOWN = 512
TN_OUT = 1024
TQ = 256
TK = 256
PAGES_PER_STEP = 8
N_PAGE_STEPS = (N_PROMPT // PAGE) // PAGES_PER_STEP
PAGE_COLS = PAGE * N_HEADS_C
LOG2E = math.log2(math.e)
SCORE_SCALE2 = ATT_SCALE * LOG2E
NEG_INF = float("-inf")

_NT = (((1,), (1,)), ((), ()))
_TN = (((0,), (0,)), ((), ()))


def _params(n_grid):
    return pltpu.CompilerParams(dimension_semantics=("arbitrary",) * n_grid,
                                vmem_limit_bytes=VMEM_LIMIT_BYTES)


def _rep(a, n):
    return jnp.concatenate([a] * n, axis=1)


def _rms_to_scratch(x_ref, g_ref, xn_ref, eps):
    x = x_ref[...]
    ms = jnp.mean(x * x, axis=-1, keepdims=True)
    xn_ref[...] = (x * lax.rsqrt(ms + eps) * g_ref[...]).astype(BF16)


def _norm_mm_body(x_ref, g_ref, *rest, starts, out_ranges, lead_scale):
    n_parts = len(starts) - 1
    w_refs, o_refs, xn_ref = rest[:n_parts], rest[n_parts:-1], rest[-1]
    j = pl.program_id(1)

    @pl.when(j == 0)
    def _():
        _rms_to_scratch(x_ref, g_ref, xn_ref, EPS)

    def store(o_ref, r):
        o_ref[...] = r.astype(o_ref.dtype)

    def emit(lo, hi, w_ref):
        r = jnp.dot(xn_ref[...], w_ref[...], preferred_element_type=F32)
        if lead_scale is not None:
            r = r * jnp.where(j < lead_scale[0], lead_scale[1], 1.0)
        for o_ref, (o_lo, o_hi) in zip(o_refs, out_ranges):
            if o_lo <= lo and hi <= o_hi:
                store(o_ref, r)
            elif o_lo < hi and lo < o_hi:
                pl.when((j >= o_lo) & (j < o_hi))(functools.partial(store, o_ref, r))

    if n_parts == 1:
        emit(starts[0], starts[1], w_refs[0])
    else:
        for p, w_ref in enumerate(w_refs):
            pl.when((j >= starts[p]) & (j < starts[p + 1]))(
                functools.partial(emit, starts[p], starts[p + 1], w_ref))


def _norm_matmul(x, g, parts, outs, tm, tn, lead_scale=None):
    m, k = x.shape
    starts = [0]
    for _, _, _, n_tiles in parts:
        starts.append(starts[-1] + n_tiles)

    def w_spec(p):
        _, layer, tile0, n_tiles = parts[p]
        lo = starts[p]
        return pl.BlockSpec((None, k, tn), lambda i, j: (layer, 0, tile0 + jnp.clip(j - lo, 0, n_tiles - 1)))

    def o_spec(lo, hi):
        return pl.BlockSpec((tm, tn), lambda i, j: (i, jnp.clip(j - lo, 0, hi - lo - 1)))

    return pl.pallas_call(
        functools.partial(_norm_mm_body, starts=tuple(starts), out_ranges=tuple((lo, hi) for _, lo, hi in outs),
                          lead_scale=lead_scale),
        grid=(m // tm, starts[-1]),
        in_specs=[pl.BlockSpec((tm, k), lambda i, j: (i, 0)),
                  pl.BlockSpec((1, k), lambda i, j: (0, 0))] + [w_spec(p) for p in range(len(parts))],
        out_specs=[o_spec(lo, hi) for _, lo, hi in outs],
        out_shape=[jax.ShapeDtypeStruct((m, (hi - lo) * tn), dt) for dt, lo, hi in outs],
        scratch_shapes=[pltpu.VMEM((tm, k), BF16)],
        compiler_params=_params(2),
        name="norm_matmul",
    )(x, g.reshape(1, k), *[w for w, _, _, _ in parts])


def _ffn_gate_body(x_ref, g_ref, wg_ref, wu_ref, o_ref, xn_ref):
    @pl.when(pl.program_id(1) == 0)
    def _():
        _rms_to_scratch(x_ref, g_ref, xn_ref, EPS)

    xn = xn_ref[...]
    a = jnp.dot(xn, wg_ref[...], preferred_element_type=F32)
    b = jnp.dot(xn, wu_ref[...], preferred_element_type=F32)
    o_ref[...] = (a * jax.nn.sigmoid(a) * b).astype(o_ref.dtype)


def _ffn_gate(x, g, wg, wu, layer, tm, tn):
    m, k = x.shape
    n = wg.shape[2]
    return pl.pallas_call(
        _ffn_gate_body,
        grid=(m // tm, n // tn),
        in_specs=[pl.BlockSpec((tm, k), lambda i, j: (i, 0)),
                  pl.BlockSpec((1, k), lambda i, j: (0, 0)),
                  pl.BlockSpec((None, k, tn), lambda i, j: (layer, 0, j)),
                  pl.BlockSpec((None, k, tn), lambda i, j: (layer, 0, j))],
        out_specs=pl.BlockSpec((tm, tn), lambda i, j: (i, j)),
        out_shape=jax.ShapeDtypeStruct((m, n), BF16),
        scratch_shapes=[pltpu.VMEM((tm, k), BF16)],
        compiler_params=_params(2),
        name="ffn_gate",
    )(x, g.reshape(1, k), wg, wu)


def _mm_res_body(y_ref, w_ref, r_ref, o_ref):
    o_ref[...] = r_ref[...] + jnp.dot(y_ref[...], w_ref[...], preferred_element_type=F32)


def _matmul_residual(y, w, layer, r, tm, tn):
    m, k = y.shape
    n = w.shape[2]
    return pl.pallas_call(
        _mm_res_body,
        grid=(m // tm, n // tn),
        in_specs=[pl.BlockSpec((tm, k), lambda i, j: (i, 0)),
                  pl.BlockSpec((None, k, tn), lambda i, j: (layer, 0, j)),
                  pl.BlockSpec((tm, tn), lambda i, j: (i, j))],
        out_specs=pl.BlockSpec((tm, tn), lambda i, j: (i, j)),
        out_shape=jax.ShapeDtypeStruct((m, n), F32),
        compiler_params=_params(2),
        name="matmul_residual",
    )(y, w, r)


def _rmsnorm_body(x_ref, g_ref, o_ref):
    x = x_ref[...]
    ms = jnp.mean(x * x, axis=-1, keepdims=True)
    o_ref[...] = x * lax.rsqrt(ms + EPS) * g_ref[...]


def _rmsnorm(x, g, tm):
    m, k = x.shape
    return pl.pallas_call(
        _rmsnorm_body,
        grid=(m // tm,),
        in_specs=[pl.BlockSpec((tm, k), lambda i: (i, 0)),
                  pl.BlockSpec((1, k), lambda i: (0, 0))],
        out_specs=pl.BlockSpec((tm, k), lambda i: (i, 0)),
        out_shape=jax.ShapeDtypeStruct((m, k), F32),
        compiler_params=_params(1),
        name="final_rmsnorm",
    )(x, g.reshape(1, k))


def _softplus(x):
    return jnp.maximum(x, 0.0) + jnp.log1p(jnp.exp(-jnp.abs(x)))


def _gelu_exact(x):
    return 0.5 * x * (1.0 + lax.erf(x * math.sqrt(0.5)))


def _pair_cols(a, h0, lo):
    return jnp.where(lo, a[:, h0:h0 + 1], a[:, h0 + 1:h0 + 2])


def _ab_body(x_ref, conv0_ref, ssm0_ref, convw_ref, convb_ref, dtb_ref, alog_ref, dskip_ref, gnw_ref,
             lng_ref, lnb_ref, ws_ref, bst_ref, *rest, lr, ch, emit_vn):
    if emit_vn:
        y_ref, ssm_out_ref, vn_ref, xp_ref, h_ref = rest
    else:
        y_ref, ssm_out_ref, xp_ref, h_ref = rest
        vn_ref = None
    c = pl.program_id(1)

    @pl.when(c == 0)
    def _():
        xp_ref[0:8, :] = conv0_ref[0]
        h_ref[...] = ssm0_ref[0]

    def pad_rows(a):
        if lr == ch:
            return a
        return jnp.concatenate([a, jnp.zeros((ch - lr, a.shape[1]), a.dtype)], axis=0)

    rows = x_ref[...]
    z = rows[:, COL_Z:COL_Z + D_INNER]
    u = rows[:, COL_U:COL_U + W_B]
    v = rows[:, COL_V:COL_V + W_B]
    dt_raw = rows[:, COL_DT:COL_DT + 128]

    xp_ref[8:8 + lr, :] = rows[:, COL_XBC:COL_XBC + CONV_DIM]
    conv = convb_ref[...]
    for tap in range(CONV_W):
        conv = conv + convw_ref[tap:tap + 1, :] * xp_ref[5 + tap:5 + tap + lr, :]
    xp_ref[0:8, :] = xp_ref[lr:lr + 8, :]
    xbc = pad_rows(conv * jax.nn.sigmoid(conv))
    xs = xbc[:, :D_INNER]
    bm = xbc[:, D_INNER:D_INNER + N_GROUPS_A * D_STATE]
    cm = xbc[:, D_INNER + N_GROUPS_A * D_STATE:]

    causal = lax.broadcasted_iota(jnp.int32, (ch, ch), 0) >= lax.broadcasted_iota(jnp.int32, (ch, ch), 1)
    lo = lax.broadcasted_iota(jnp.int32, (ch, 128), 1) < HEADDIM_A
    head_lane = lax.broadcasted_iota(jnp.int32, (1, 128), 1) < N_HEADS_A

    a_neg = jnp.where(head_lane, -jnp.exp(alog_ref[...]), 0.0)
    dtv = pad_rows(_softplus(dt_raw + dtb_ref[...]))
    a_cum = jnp.dot(causal.astype(F32), dtv * a_neg, preferred_element_type=F32,
                    precision=lax.Precision.HIGHEST)
    a_cum_t = a_cum.T
    a_last = a_cum[ch - 1:ch, :]
    decay_end = jnp.exp(a_last - a_cum)
    decay_in = jnp.exp(a_cum)
    chunk_decay = jnp.exp(a_last)

    y_parts = []
    for g in range(N_GROUPS_A):
        bg = bm[:, g * D_STATE:(g + 1) * D_STATE].astype(BF16)
        cg = cm[:, g * D_STATE:(g + 1) * D_STATE].astype(BF16)
        cb = lax.dot_general(cg, bg, _NT, preferred_element_type=F32)
        rows_g = slice(g * 8 * HEADDIM_A, (g + 1) * 8 * HEADDIM_A)
        y_off = lax.dot_general(cg, h_ref[rows_g, :].astype(BF16), _NT,
                                preferred_element_type=F32)
        xdtd_parts = []
        for jj in range(4):
            j = g * 4 + jj
            h0 = 2 * j
            xdt = xs[:, j * 128:(j + 1) * 128] * _pair_cols(dtv, h0, lo)
            m_pair = []
            for h in (h0, h0 + 1):
                seg = a_cum[:, h:h + 1] - a_cum_t[h:h + 1, :]
                m_pair.append((cb * jnp.exp(jnp.where(causal, seg, NEG_INF))).astype(BF16))
            lhs = jnp.concatenate(m_pair, axis=1)
            rhs = jnp.concatenate([jnp.where(lo, xdt, 0.0).astype(BF16),
                                   jnp.where(lo, 0.0, xdt).astype(BF16)], axis=0)
            y_diag = jnp.dot(lhs, rhs, preferred_element_type=F32)
            y_parts.append(y_diag + y_off[:, jj * 128:(jj + 1) * 128] * _pair_cols(decay_in, h0, lo))
            xdtd_parts.append((xdt * _pair_cols(decay_end, h0, lo)).astype(BF16))
        xdtd = jnp.concatenate(xdtd_parts, axis=1)
        st = lax.dot_general(xdtd, bg, _TN, preferred_element_type=F32)
        for e in range(8):
            h = g * 8 + e
            r0 = h * HEADDIM_A
            h_ref[r0:r0 + HEADDIM_A, :] = (h_ref[r0:r0 + HEADDIM_A, :] * chunk_decay[0:1, h:h + 1]
                                           + st[e * HEADDIM_A:(e + 1) * HEADDIM_A, :])

    @pl.when(c == pl.num_programs(1) - 1)
    def _():
        ssm_out_ref[0] = h_ref[...]

    y = (jnp.concatenate(y_parts, axis=1) + dskip_ref[...] * xs)[:lr]
    gated = y * (z * jax.nn.sigmoid(z))
    ya_parts = []
    gw = D_INNER // N_GROUPS_A
    for g in range(N_GROUPS_A):
        gg = gated[:, g * gw:(g + 1) * gw]
        ms = jnp.mean(gg * gg, axis=-1, keepdims=True)
        ya_parts.append(gg * lax.rsqrt(ms + 1e-5) * gnw_ref[:, g * gw:(g + 1) * gw])

    ug = _gelu_exact(u)
    vg = _gelu_exact(v)
    mu = jnp.mean(vg, axis=-1, keepdims=True)
    xc = vg - mu
    var = jnp.mean(xc * xc, axis=-1, keepdims=True)
    vn = xc * lax.rsqrt(var + 1e-5) * lng_ref[...] + lnb_ref[...]
    vn16 = pad_rows(vn).astype(BF16)
    wb = W_B // N_GROUPS_B
    yb_parts = []
    for g in range(N_GROUPS_B):
        w_causal = jnp.where(causal, ws_ref[g, :ch, :ch], 0.0).astype(BF16)
        mix = jnp.dot(w_causal, vn16[:, g * wb:(g + 1) * wb], preferred_element_type=F32)
        yb_parts.append(ug[:, g * wb:(g + 1) * wb] * (mix + bst_ref[:ch, g:g + 1])[:lr])

    y_ref[...] = jnp.concatenate(ya_parts + yb_parts, axis=1).astype(y_ref.dtype)
    if emit_vn:
        vn_ref[...] = vn


def _ab_mixer(proj, conv0, ssm0, p, *, n_seq, n_chunk, lr, out_dtype, emit_vn):
    n_out_rows = n_seq * n_chunk * lr
    row_map = lambda s, c: (s * n_chunk + c, 0)
    out_map = row_map
    const2 = lambda s, c: (0, 0)
    out_specs = [pl.BlockSpec((lr, 2 * D_INNER), out_map),
                 pl.BlockSpec((1, N_HEADS_A * HEADDIM_A, D_STATE), lambda s, c: (s, 0, 0))]
    out_shape = [jax.ShapeDtypeStruct((n_out_rows, 2 * D_INNER), out_dtype),
                 jax.ShapeDtypeStruct((n_seq, N_HEADS_A * HEADDIM_A, D_STATE), F32)]
    if emit_vn:
        out_specs.append(pl.BlockSpec((lr, W_B), out_map))
        out_shape.append(jax.ShapeDtypeStruct((n_out_rows, W_B), F32))
    return pl.pallas_call(
        functools.partial(_ab_body, lr=lr, ch=max(lr, SCAN_ROWS_MIN), emit_vn=emit_vn),
        grid=(n_seq, n_chunk),
        in_specs=[pl.BlockSpec((lr, N_PROJ), row_map),
                  pl.BlockSpec((1, 8, CONV_DIM), lambda s, c: (s, 0, 0)),
                  pl.BlockSpec((1, N_HEADS_A * HEADDIM_A, D_STATE), lambda s, c: (s, 0, 0)),
                  pl.BlockSpec((CONV_W, CONV_DIM), const2),
                  pl.BlockSpec((1, CONV_DIM), const2),
                  pl.BlockSpec((1, 128), const2),
                  pl.BlockSpec((1, 128), const2),
                  pl.BlockSpec((1, D_INNER), const2),
                  pl.BlockSpec((1, D_INNER), const2),
                  pl.BlockSpec((1, W_B), const2),
                  pl.BlockSpec((1, W_B), const2),
                  pl.BlockSpec((N_GROUPS_B, CHUNK, CHUNK), lambda s, c: (0, 0, 0)),
                  pl.BlockSpec((CHUNK, N_GROUPS_B), const2)],
        out_specs=out_specs,
        out_shape=out_shape,
        scratch_shapes=[pltpu.VMEM((8 + CHUNK, CONV_DIM), F32),
                        pltpu.VMEM((N_HEADS_A * HEADDIM_A, D_STATE), F32)],
        compiler_params=_params(2),
        name="ab_mixer",
    )(proj, conv0, ssm0, p["conv_w"], p["conv_b"], p["dt_bias"], p["a_log"], p["d_skip"], p["gnorm_w"],
      p["ln_v_g"], p["ln_v_b"], p["w_spatial"], p["b_spatial_t"])


def _t5_bucket(rel):
    n = jnp.maximum(rel, 0)
    max_exact = NUM_BUCKETS // 2
    nf = jnp.maximum(n, 1).astype(F32)
    large = max_exact + (jnp.log(nf / max_exact) / math.log(MAX_DISTANCE / max_exact)
                         * (NUM_BUCKETS - max_exact)).astype(jnp.int32)
    large = jnp.minimum(large, NUM_BUCKETS - 1)
    return jnp.where(n < max_exact, n, large)


def _bias_body(tab_ref, pt_ref, far_ref, last_ref, own_ref):
    h = pl.program_id(0)

    def lookup(rel):
        bucket = _t5_bucket(rel)
        val = jnp.zeros(rel.shape, F32)
        for b in range(NUM_BUCKETS):
            val = jnp.where(bucket == b, tab_ref[b, h], val)
        return jnp.where(rel >= 0, val * LOG2E, NEG_INF)

    r = lax.broadcasted_iota(jnp.int32, (TQ, TK), 0)
    c = lax.broadcasted_iota(jnp.int32, (TQ, TK), 1)
    pt_ref[0, 0] = jnp.full((TQ, TK), NEG_INF, F32)
    for d in range(2):
        pt_ref[0, 1 + d] = lookup(d * TK + r - c)
    pt_ref[0, 3] = jnp.full((TQ, TK), tab_ref[NUM_BUCKETS - 1, h] * LOG2E, F32)

    t = lax.broadcasted_iota(jnp.int32, (2 * T_S, PAGE_COLS), 0) & (T_S - 1)
    col = lax.broadcasted_iota(jnp.int32, (2 * T_S, PAGE_COLS), 1)
    pos = lax.shift_right_logical(col, 3)
    mine = (col & (N_HEADS_C - 1)) == h
    far_ref[...] = jnp.where(mine, tab_ref[NUM_BUCKETS - 1, h] * LOG2E, NEG_INF)
    last_ref[...] = jnp.where(mine, lookup(PAGE + t - pos), NEG_INF)
    t_own = lax.broadcasted_iota(jnp.int32, (2 * T_S, 128), 0) & (T_S - 1)
    col_own = lax.broadcasted_iota(jnp.int32, (2 * T_S, 128), 1)
    own = lookup(t_own - lax.shift_right_logical(col_own, 3))
    own = jnp.where((col_own & (N_HEADS_C - 1)) == h, own, NEG_INF)
    own_ref[...] = jnp.where(col_own < T_S * N_HEADS_C, own, NEG_INF)


def _bias_tiles(rel_bias):
    rows = lambda n: pl.BlockSpec((2 * T_S, n), lambda h: (h, 0))
    return pl.pallas_call(
        _bias_body,
        grid=(N_HEADS_C,),
        in_specs=[pl.BlockSpec(memory_space=pltpu.SMEM)],
        out_specs=[pl.BlockSpec((1, 4, TQ, TK), lambda h: (h, 0, 0, 0)),
                   rows(PAGE_COLS), rows(PAGE_COLS), rows(128)],
        out_shape=[jax.ShapeDtypeStruct((N_HEADS_C, 4, TQ, TK), F32),
                   jax.ShapeDtypeStruct((128, PAGE_COLS), F32),
                   jax.ShapeDtypeStruct((128, PAGE_COLS), F32),
                   jax.ShapeDtypeStruct((128, 128), F32)],
        compiler_params=_params(1),
        name="t5_bias_tiles",
    )(rel_bias)


def _lambda(lq1_ref, lk1_ref, lq2_ref, lk2_ref, lam_init):
    s1 = jnp.sum(lq1_ref[...] * lk1_ref[...], axis=-1, keepdims=True)
    s2 = jnp.sum(lq2_ref[...] * lk2_ref[...], axis=-1, keepdims=True)
    return jnp.exp(s1) - jnp.exp(s2) + lam_init


def _subln(o, w, lam_init):
    ms = jnp.mean(o * o, axis=-1, keepdims=True)
    return o * lax.rsqrt(ms + 1e-5) * w * (1.0 - lam_init)


def _softmax_step(s2, v16, m_ref, l_ref, acc_ref, idx):
    m_new = jnp.maximum(m_ref[idx], jnp.max(s2, axis=-1, keepdims=True))
    alpha = jnp.exp2(m_ref[idx] - m_new)
    p = jnp.exp2(s2 - _rep(m_new, s2.shape[1] // 128))
    l_ref[idx] = alpha * l_ref[idx] + jnp.sum(p, axis=-1, keepdims=True)
    acc_ref[idx] = (acc_ref[idx] * _rep(alpha, acc_ref.shape[-1] // 128)
                    + jnp.dot(p.astype(BF16), v16, preferred_element_type=F32))
    m_ref[idx] = m_new


def _pattn_compute(qi, q_ref, k_ref, v_ref, bias_ref, lam_refs, subw_ref, o_ref, s_ref, m_ref, l_ref, acc_ref,
                   lam_init):
    q = q_ref[...]
    qs = (q[:, :HEAD_DIM_C], q[:, HEAD_DIM_C:])
    m_ref[...] = jnp.full(m_ref.shape, NEG_INF, F32)
    l_ref[...] = jnp.zeros(l_ref.shape, F32)
    acc_ref[...] = jnp.zeros(acc_ref.shape, F32)
    n_chunks = lax.shift_right_logical(qi, 1) + 1
    chunk = 2 * TK
    last_chunk = N_PROMPT // chunk - 1

    def rows_of(c):
        return pl.ds(pl.multiple_of(jnp.minimum(c, last_chunk) * chunk, chunk), chunk)

    def put_scores(c, slot):
        k = k_ref[rows_of(c), :]
        for mp in range(2):
            s_ref[slot, mp] = lax.dot_general(qs[mp], k[:, mp * HEAD_DIM_C:(mp + 1) * HEAD_DIM_C], _NT,
                                              preferred_element_type=F32)

    def half(c, slot):
        put_scores(c + 1, 1 - slot)
        v = v_ref[rows_of(c), :]
        kind_l = jnp.clip(qi - 2 * c + 1, 0, 3)
        kind_r = jnp.clip(qi - 2 * c, 0, 3)
        bias2 = jnp.concatenate([bias_ref[0, kind_l], bias_ref[0, kind_r]], axis=1)
        for mp in range(2):
            _softmax_step(s_ref[slot, mp] + bias2, v, m_ref, l_ref, acc_ref, mp)

    put_scores(0, 0)

    def body(i, carry):
        half(2 * i, 0)
        half(2 * i + 1, 1)
        return carry

    lax.fori_loop(0, lax.shift_right_logical(n_chunks + 1, 1), body, 0)

    lam = _lambda(*lam_refs, lam_init)
    n_rep = 2 * HEAD_DIM_C // 128
    o = (acc_ref[0] * _rep(1.0 / l_ref[0], n_rep)
         - lam * (acc_ref[1] * _rep(1.0 / l_ref[1], n_rep)))
    o_ref[...] = _subln(o, subw_ref[...], lam_init).astype(o_ref.dtype)


def _sattn_compute(j, q_ref, ko_ref, vo_ref, k_refs, v_refs, far_ref, last_ref, own_ref, lam_refs, subw_ref,
                   o_ref, qs_ref, m_ref, l_ref, acc_ref, al_ref, sc_ref, lam_init):
    dh2 = 2 * HEAD_DIM_C

    @pl.when(j == 0)
    def _():
        q = q_ref[...]
        first_map = lax.broadcasted_iota(jnp.int32, (T_S, dh2), 1) < HEAD_DIM_C
        blocks = []
        for h in range(N_HEADS_C):
            q_h = q[:, h * dh2:(h + 1) * dh2]
            blocks += [jnp.where(first_map, q_h, 0.0), jnp.where(first_map, 0.0, q_h)]
        qs_ref[...] = jnp.concatenate(blocks, axis=0).astype(BF16)
        m_ref[...] = jnp.full(m_ref.shape, NEG_INF, F32)
        l_ref[...] = jnp.zeros(l_ref.shape, F32)
        acc_ref[...] = jnp.zeros(acc_ref.shape, F32)

    def scores(k16, bias2):
        return lax.dot_general(qs_ref[...], k16, _NT, preferred_element_type=F32) + bias2

    def step(k16, v16, bias2):
        _softmax_step(scores(k16, bias2), v16, m_ref, l_ref, acc_ref, 0)

    is_last = j == N_PAGE_STEPS - 1
    m_cur = None
    for i in range(PAGES_PER_STEP):
        bias2 = far_ref[...]
        if i == PAGES_PER_STEP - 1:
            bias2 = jnp.where(is_last, last_ref[...], bias2)
        s2 = scores(k_refs[i][0].astype(BF16), bias2)
        sc_ref[i] = s2
        m_i = jnp.max(s2, axis=-1, keepdims=True)
        m_cur = m_i if m_cur is None else jnp.maximum(m_cur, m_i)
    m_new = jnp.maximum(m_ref[0], m_cur)
    al_ref[0] = jnp.exp2(m_ref[0] - m_new)
    m_ref[0] = m_new

    @pl.when(j >= 0)
    def _():
        alpha = al_ref[0]
        shift = _rep(m_ref[0], PAGE_COLS // 128)
        l_new = alpha * l_ref[0]
        acc_new = acc_ref[0] * _rep(alpha, dh2 // 128)
        for i in range(PAGES_PER_STEP):
            p = jnp.exp2(sc_ref[i] - shift)
            l_new = l_new + jnp.sum(p, axis=-1, keepdims=True)
            acc_new = acc_new + jnp.dot(p.astype(BF16), v_refs[i][0].astype(BF16), preferred_element_type=F32)
        l_ref[0] = l_new
        acc_ref[0] = acc_new

    @pl.when(is_last)
    def _():
        pad = jnp.zeros((128 - T_S * N_HEADS_C, dh2), F32)
        step(jnp.concatenate([ko_ref[0], pad], axis=0).astype(BF16),
             jnp.concatenate([vo_ref[0], pad], axis=0).astype(BF16), own_ref[...])
        lam = _lambda(*lam_refs, lam_init)
        o_all = acc_ref[0] * _rep(1.0 / l_ref[0], dh2 // 128)
        for h in range(N_HEADS_C):
            r0 = h * 2 * T_S
            o_h = o_all[r0:r0 + T_S] - lam * o_all[r0 + T_S:r0 + 2 * T_S]
            o_ref[0, :, h * dh2:(h + 1) * dh2] = _subln(o_h, subw_ref[...], lam_init)


def _attn_body(pt_ref, q_ref, k_ref, v_ref, bias_ref, qsm_ref, ko_ref, vo_ref, *rest, lam_init):
    k_refs = rest[:PAGES_PER_STEP]
    v_refs = rest[PAGES_PER_STEP:2 * PAGES_PER_STEP]
    (far_ref, last_ref, own_ref, lq1_ref, lk1_ref, lq2_ref, lk2_ref, subw_ref, op_ref, os_ref,
     s_ref, m_ref, l_ref, acc_ref, qs_ref, ms_ref, ls_ref, accs_ref, als_ref, sc_ref) = rest[2 * PAGES_PER_STEP:]
    del pt_ref
    lam_refs = (lq1_ref, lk1_ref, lq2_ref, lk2_ref)
    qi = pl.program_id(1)
    t = pl.program_id(0) * pl.num_programs(1) + qi
    _sattn_compute(t & (N_PAGE_STEPS - 1), qsm_ref, ko_ref, vo_ref, k_refs, v_refs, far_ref, last_ref, own_ref,
                   lam_refs, subw_ref, os_ref, qs_ref, ms_ref, ls_ref, accs_ref, als_ref, sc_ref, lam_init)
    _pattn_compute(qi, q_ref, k_ref, v_ref, bias_ref, lam_refs, subw_ref, op_ref, s_ref, m_ref, l_ref, acc_ref,
                   lam_init)


def _attention(qkv16, bias_pt, page_table, q_s, k_own, v_own, cache_k, cache_v, far, last, own, p, lam_init):
    dh2 = 2 * HEAD_DIM_C
    n_q = N_PROMPT // TQ
    assert N_HEADS_C * n_q == N_SEQ_S * N_PAGE_STEPS and N_PAGE_STEPS & (N_PAGE_STEPS - 1) == 0
    seq = lambda h, i: (h * n_q + i) // N_PAGE_STEPS
    grp = lambda h, i: (h * n_q + i) % N_PAGE_STEPS
    resident = lambda col0: pl.BlockSpec((N_PROMPT, dh2), lambda h, i, pt: (0, col0 + h),
                                         pipeline_mode=pl.Buffered(1))
    own_spec = pl.BlockSpec((1, T_S * N_HEADS_C, dh2), lambda h, i, pt: (seq(h, i), 0, 0))
    page_spec = lambda n: pl.BlockSpec(
        (1, PAGE_COLS, dh2), lambda h, i, pt: (pt[seq(h, i), grp(h, i) * PAGES_PER_STEP + n], 0, 0))
    const = lambda shape: pl.BlockSpec(shape, lambda h, i, pt: (0, 0))
    grid_spec = pltpu.PrefetchScalarGridSpec(
        num_scalar_prefetch=1,
        grid=(N_HEADS_C, n_q),
        in_specs=([pl.BlockSpec((TQ, dh2), lambda h, i, pt: (i, h)), resident(N_HEADS_C), resident(2 * N_HEADS_C),
                   pl.BlockSpec((1, 4, TQ, TK), lambda h, i, pt: (h, 0, 0, 0)),
                   pl.BlockSpec((T_S, ATT_W), lambda h, i, pt: (seq(h, i), 0)), own_spec, own_spec]
                  + [page_spec(n) for n in range(PAGES_PER_STEP)]
                  + [page_spec(n) for n in range(PAGES_PER_STEP)]
                  + [const((128, PAGE_COLS)), const((128, PAGE_COLS)), const((128, 128)),
                     const((1, HEAD_DIM_C)), const((1, HEAD_DIM_C)), const((1, HEAD_DIM_C)),
                     const((1, HEAD_DIM_C)), const((1, dh2))]),
        out_specs=[pl.BlockSpec((TQ, dh2), lambda h, i, pt: (i, h)),
                   pl.BlockSpec((1, T_S, ATT_W), lambda h, i, pt: (seq(h, i), 0, 0))],
        scratch_shapes=[pltpu.VMEM((2, 2, TQ, 2 * TK), F32),
                        pltpu.VMEM((2, TQ, 128), F32),
                        pltpu.VMEM((2, TQ, 128), F32),
                        pltpu.VMEM((2, TQ, dh2), F32),
                        pltpu.VMEM((128, dh2), BF16),
                        pltpu.VMEM((1, 128, 128), F32),
                        pltpu.VMEM((1, 128, 128), F32),
                        pltpu.VMEM((1, 128, dh2), F32),
                        pltpu.VMEM((1, 128, 128), F32),
                        pltpu.VMEM((PAGES_PER_STEP, 128, PAGE_COLS), F32)],
    )
    return pl.pallas_call(
        functools.partial(_attn_body, lam_init=lam_init),
        grid_spec=grid_spec,
        out_shape=[jax.ShapeDtypeStruct((N_PROMPT, ATT_W), BF16),
                   jax.ShapeDtypeStruct((N_SEQ_S, T_S, ATT_W), F32)],
        compiler_params=_params(2),
        name="attention",
    )(page_table, qkv16, qkv16, qkv16, bias_pt, q_s, k_own, v_own,
      *([cache_k] * PAGES_PER_STEP), *([cache_v] * PAGES_PER_STEP),
      far, last, own, p["lq1"], p["lk1"], p["lq2"], p["lk2"], p["subln_w"])


def kernel(x_prompt, x_sample, cache_k, cache_v, page_table, state_conv, state_ssm, norm_mix, norm_ffn, norm_final, w_in_ab, conv_w, conv_b, dt_bias, a_log, d_skip, gnorm_w, ln_v_g, ln_v_b, w_spatial, b_spatial, w_out_ab, w_qkv, lambda_q1, lambda_k1, lambda_q2, lambda_k2, subln_w, w_o, rel_bias, w_gate, w_up, w_down):
    w_in16 = w_in_ab.astype(BF16)
    n_zx = D_INNER + CONV_DIM
    w_uv16 = w_in16[:, :, n_zx + N_HEADS_A:]
    w_dt16 = jnp.pad(w_in16[:, :, n_zx:n_zx + N_HEADS_A], ((0, 0), (0, 0), (0, TN_IN - N_HEADS_A)))
    w_in_parts = [(w_in16, 0, 0, n_zx // TN_IN), (w_uv16, 0, 0, 2 * W_B // TN_IN), (w_dt16, 0, 0, 1)]
    w_out16, w_qkv16, w_o16 = w_out_ab.astype(BF16), w_qkv.astype(BF16), w_o.astype(BF16)
    w_gate16, w_up16, w_down16 = w_gate.astype(BF16), w_up.astype(BF16), w_down.astype(BF16)
    qkv_tiles = ATT_W // TN_QKV
    qkv_outs = [(BF16, 0, 3 * qkv_tiles), (F32, qkv_tiles, 2 * qkv_tiles), (F32, 2 * qkv_tiles, 3 * qkv_tiles)]

    pad_heads = lambda a: jnp.pad(a.reshape(1, N_HEADS_A), ((0, 0), (0, 128 - N_HEADS_A)))
    ab_params = dict(
        conv_w=conv_w[0], conv_b=conv_b[0].reshape(1, CONV_DIM),
        dt_bias=pad_heads(dt_bias[0]), a_log=pad_heads(a_log[0]),
        d_skip=jnp.repeat(d_skip[0], HEADDIM_A).reshape(1, D_INNER),
        gnorm_w=gnorm_w[0].reshape(1, D_INNER),
        ln_v_g=ln_v_g[0].reshape(1, W_B), ln_v_b=ln_v_b[0].reshape(1, W_B),
        w_spatial=w_spatial[0], b_spatial_t=b_spatial[0].T)
    att_params = dict(lq1=lambda_q1[0].reshape(1, HEAD_DIM_C), lk1=lambda_k1[0].reshape(1, HEAD_DIM_C),
                      lq2=lambda_q2[0].reshape(1, HEAD_DIM_C), lk2=lambda_k2[0].reshape(1, HEAD_DIM_C),
                      subln_w=subln_w[0].reshape(1, 2 * HEAD_DIM_C))
    lam_init = 0.8 - 0.6 * math.exp(-0.3 * 1)
    bias_pt, bias_far, bias_last, bias_own = _bias_tiles(rel_bias)
    n_state_rows = N_HEADS_A * HEADDIM_A

    def layer0(x, tm, mixer):
        (proj,) = _norm_matmul(x, norm_mix[0], w_in_parts, [(F32, 0, N_PROJ // TN_IN)], tm, TN_IN)
        y_ab, *state = mixer(proj)
        h = _matmul_residual(y_ab, w_out16, 0, x, tm, TN_OUT)
        h = _matmul_residual(_ffn_gate(h, norm_ffn[0], w_gate16, w_up16, 0, tm, TN_FFN), w_down16, 0, h, tm, TN_DOWN)
        return proj, state, h

    def qkv_stage(h, tm):
        return _norm_matmul(h, norm_mix[1], [(w_qkv16, 0, 0, 3 * qkv_tiles)], qkv_outs, tm, TN_QKV,
                            lead_scale=(qkv_tiles, SCORE_SCALE2))

    def layer1_tail(h, o, tm):
        h = _matmul_residual(o, w_o16, 0, h, tm, TN_OUT)
        h = _matmul_residual(_ffn_gate(h, norm_ffn[1], w_gate16, w_up16, 1, tm, TN_FFN), w_down16, 1, h, tm, TN_DOWN)
        return _rmsnorm(h, norm_final, tm)

    def mixer_p(proj):
        return _ab_mixer(proj, jnp.zeros((1, 8, CONV_DIM), F32), jnp.zeros((1, n_state_rows, D_STATE), F32),
                         ab_params, n_seq=1, n_chunk=N_PROMPT // CHUNK, lr=CHUNK, out_dtype=BF16, emit_vn=False)

    def mixer_s(proj):
        conv0 = jnp.pad(state_conv[0], ((0, 0), (8 - (CONV_W - 1), 0), (0, 0)))
        y, ssm, vn = _ab_mixer(proj, conv0, state_ssm[0].reshape(N_SEQ_S, n_state_rows, D_STATE), ab_params,
                               n_seq=N_SEQ_S, n_chunk=1, lr=T_S, out_dtype=F32, emit_vn=True)
        return y.astype(BF16), ssm, vn

    proj_p, (ssm_p,), h_p = layer0(x_prompt.reshape(N_PROMPT, D_MODEL), TM_PROMPT, mixer_p)
    proj_s, (ssm_s, vn_s), h_s = layer0(x_sample.reshape(N_SAMPLE, D_MODEL), TM_SAMPLE, mixer_s)
    qkv16_p, k_p, v_p = qkv_stage(h_p, TM_PROMPT)
    qkv16_s, k_s, v_s = qkv_stage(h_s, TM_SAMPLE)
    page_shape = (cache_k.shape[1], PAGE_COLS, 2 * HEAD_DIM_C)
    own_shape = (N_SEQ_S, T_S * N_HEADS_C, 2 * HEAD_DIM_C)
    o_p, o_s = _attention(qkv16_p, bias_pt, page_table, qkv16_s[:, :ATT_W].astype(F32), k_s.reshape(own_shape),
                          v_s.reshape(own_shape), cache_k[0].reshape(page_shape), cache_v[0].reshape(page_shape),
                          bias_far, bias_last, bias_own, att_params, lam_init)
    y_p = layer1_tail(h_p, o_p, TM_PROMPT)
    y_s = layer1_tail(h_s, o_s.reshape(N_SAMPLE, ATT_W).astype(BF16), TM_SAMPLE)

    xbc_cols = slice(COL_XBC, COL_XBC + CONV_DIM)
    kv_shape_p = (1, 1, N_PROMPT, N_HEADS_C, 2 * HEAD_DIM_C)
    kv_shape_s = (1, N_SEQ_S, T_S, N_HEADS_C, 2 * HEAD_DIM_C)
    return (
        y_p.reshape(1, N_PROMPT, D_MODEL),
        y_s.reshape(N_SEQ_S, T_S, D_MODEL),
        proj_p[N_PROMPT - (CONV_W - 1):, xbc_cols].reshape(1, 1, CONV_W - 1, CONV_DIM),
        ssm_p.reshape(1, 1, N_HEADS_A, HEADDIM_A, D_STATE),
        k_p.reshape(kv_shape_p),
        v_p.reshape(kv_shape_p),
        proj_s.reshape(N_SEQ_S, T_S, N_PROJ)[:, T_S - (CONV_W - 1):, xbc_cols].reshape(
            1, N_SEQ_S, CONV_W - 1, CONV_DIM),
        ssm_s.reshape(1, N_SEQ_S, N_HEADS_A, HEADDIM_A, D_STATE),
        vn_s.reshape(1, N_SEQ_S, T_S, W_B),
        k_s.reshape(kv_shape_s),
        v_s.reshape(kv_shape_s),
    )
```
